```python
import jax
import jax.numpy as jnp
from jax import lax
import numpy as np


D_MODEL = 1024
BATCH = 16
SEQ = 4096
DEPTH = 2
DEC_BATCH = 32
DEC_SEQ = 32
PAST_LEN = 2048

CHUNK = 64
PLE_DIM = 256
D_BRANCH = D_MODEL // 2
CONV_A = 3
CONV_B = 31
C_HEADS = 4
C_CHUNK = 128
D_HEADS = 8
D_HEAD_DIM = D_BRANCH // D_HEADS
Q_BLOCK = 128
N_AB_LAYERS = (DEPTH + 1) // 2
N_CD_LAYERS = DEPTH // 2
AB_IN = 7 * D_BRANCH
CD_IN = 7 * D_BRANCH
EPS = 1e-6

kernel_name = 'hybrid_conv_sgmlp_stickbreak_stream_step'


def _rmsnorm(x, g):
    x32 = x.astype(jnp.float32)
    y = x32 * lax.rsqrt(jnp.mean(x32 * x32, axis=-1, keepdims=True) + EPS)
    return (y * g.astype(jnp.float32)).astype(x.dtype)


def _layernorm(x, g, b):
    x32 = x.astype(jnp.float32)
    mu = jnp.mean(x32, axis=-1, keepdims=True)
    xc = x32 - mu
    y = xc * lax.rsqrt(jnp.mean(xc * xc, axis=-1, keepdims=True) + EPS)
    return (y * g.astype(jnp.float32) + b.astype(jnp.float32)).astype(x.dtype)


def _causal_dwconv(u, hist, w):
    full = jnp.concatenate([hist.astype(u.dtype), u], axis=1)
    out = lax.conv_general_dilated(
        full, w[:, None, :].astype(u.dtype), window_strides=(1,), padding='VALID',
        dimension_numbers=('NWC', 'WIO', 'NWC'), feature_group_count=u.shape[-1])
    return out, full[:, -(w.shape[0] - 1):]


def _chunk_mix(v, ws, bias):
    bn, t, _ = v.shape
    length = min(t, C_CHUNK)
    n = t // length
    vh = v.reshape(bn, n, length, C_HEADS, D_BRANCH // C_HEADS)
    w = jnp.tril(ws[:, :length, :length]).astype(v.dtype)
    b = jnp.transpose(bias[:, :length])[:, :, None].astype(v.dtype)
    mixed = jnp.einsum('hts,bnshc->bnthc', w, vh) + b
    return mixed.reshape(bn, t, D_BRANCH)


def _sb_block(q, q_pos, k, v, k_pos):
    z = jnp.einsum('bqhd,bkhd->bhqk', q, k).astype(jnp.float32) * (D_HEAD_DIM ** -0.5)
    mask = k_pos[None, :] < q_pos[:, None]
    log_keep = jnp.where(mask, jax.nn.log_sigmoid(-z), 0.0)
    log_w = jax.nn.log_sigmoid(z) + lax.cumsum(log_keep, axis=3, reverse=True) - log_keep
    a = jnp.where(mask, jnp.exp(log_w), 0.0)
    return jnp.einsum('bhqk,bkhd->bqhd', a.astype(v.dtype), v)


def _sb_attention(q, k, v, q_start):
    tq = q.shape[1]
    k_pos = jnp.arange(k.shape[1], dtype=jnp.int32)
    q_pos = q_start + jnp.arange(tq, dtype=jnp.int32)
    if tq > Q_BLOCK and tq % Q_BLOCK == 0:
        nb = tq // Q_BLOCK
        qb = jnp.moveaxis(q.reshape(q.shape[0], nb, Q_BLOCK, D_HEADS, D_HEAD_DIM), 1, 0)
        pb = q_pos.reshape(nb, Q_BLOCK)
        ob = lax.map(lambda qp: _sb_block(qp[0], qp[1], k, v, k_pos), (qb, pb))
        return jnp.moveaxis(ob, 0, 1).reshape(q.shape)
    return _sb_block(q, q_pos, k, v, k_pos)


def _ab_mixer(xn, hist_a, hist_b, w_in, a_conv_w, b_conv_w, b_ln_g, b_ln_b, w_out):
    proj = xn @ w_in
    a_x, a_c, a_b, a_z, b_val, b_glu, b_z = jnp.split(proj, 7, axis=-1)
    a_conv, new_a = _causal_dwconv(a_c * a_x, hist_a, a_conv_w)
    a_out = a_b * a_conv * jax.nn.silu(a_z)
    b_conv, new_b = _causal_dwconv(b_val * jax.nn.sigmoid(b_glu), hist_b, b_conv_w)
    b_out = jax.nn.silu(_layernorm(b_conv, b_ln_g, b_ln_b)) * jax.nn.silu(b_z)
    y = jnp.concatenate([a_out, b_out], axis=-1) @ w_out
    return y, new_a, new_b


def _cd_mixer(xn, k_past, v_past, q_start, w_in, c_ln_g, c_ln_b, c_ws, c_b, w_out):
    bn, t, _ = xn.shape
    proj = xn @ w_in
    c_u, c_v, c_z, d_q, d_k, d_v, d_z = jnp.split(proj, 7, axis=-1)
    c_vn = _layernorm(c_v, c_ln_g, c_ln_b)
    c_out = c_u * _chunk_mix(c_vn, c_ws, c_b) * jax.nn.silu(c_z)
    q = d_q.reshape(bn, t, D_HEADS, D_HEAD_DIM)
    k = d_k.reshape(bn, t, D_HEADS, D_HEAD_DIM)
    v = d_v.reshape(bn, t, D_HEADS, D_HEAD_DIM)
    if k_past is None:
        k_all, v_all = k, v
    else:
        k_all = jnp.concatenate([k_past.astype(k.dtype), k], axis=1)
        v_all = jnp.concatenate([v_past.astype(v.dtype), v], axis=1)
    o = _sb_attention(q, k_all, v_all, q_start).reshape(bn, t, D_BRANCH)
    d_out = o * jax.nn.silu(d_z)
    y = jnp.concatenate([c_out, d_out], axis=-1) @ w_out
    return y, c_vn, k, v


def _trunk(h, p, a_hist, b_hist, k_past, v_past, q_start, norm_g, ple_gate, ple_proj,
           ab_w_in, a_conv_w, b_conv_w, b_ln_g, b_ln_b, ab_w_out,
           cd_w_in, c_ln_g, c_ln_b, c_ws, c_b, cd_w_out, final_g):
    new_a, new_b, new_cv, new_k, new_v = [], [], [], [], []
    for i in range(DEPTH):
        j = i // 2
        xn = _rmsnorm(h, norm_g[i])
        if i % 2 == 0:
            y, sa, sb = _ab_mixer(xn, a_hist[j], b_hist[j], ab_w_in[j], a_conv_w[j],
                                  b_conv_w[j], b_ln_g[j], b_ln_b[j], ab_w_out[j])
            new_a.append(sa)
            new_b.append(sb)
        else:
            kp = None if k_past is None else k_past[j]
            vp = None if v_past is None else v_past[j]
            y, cv, kn, vn = _cd_mixer(xn, kp, vp, q_start, cd_w_in[j], c_ln_g[j], c_ln_b[j],
                                      c_ws[j], c_b[j], cd_w_out[j])
            new_cv.append(cv)
            new_k.append(kn)
            new_v.append(vn)
        h = h + y
        h = h + jax.nn.sigmoid(h @ ple_gate[i]) * (p[i].astype(h.dtype) @ ple_proj[i])
    return _rmsnorm(h, final_g), new_a, new_b, new_cv, new_k, new_v


def setup_inputs(seed: int = 0) -> dict:
    key = jax.random.key(seed)
    ks = jax.random.split(key, 32)
    f32 = jnp.float32

    def nrm(k, shape, scale=1.0):
        return jax.random.normal(k, shape, f32) * scale

    return {
        'x_prompt': nrm(ks[0], (BATCH, SEQ, D_MODEL)),
        'x_sample': nrm(ks[1], (DEC_BATCH, DEC_SEQ, D_MODEL)),
        'state_a_conv': nrm(ks[2], (N_AB_LAYERS, DEC_BATCH, CONV_A - 1, D_BRANCH)),
        'state_b_conv': nrm(ks[3], (N_AB_LAYERS, DEC_BATCH, CONV_B - 1, D_BRANCH)),
        'cache_d_k': nrm(ks[4], (N_CD_LAYERS, DEC_BATCH, PAST_LEN, D_HEADS, D_HEAD_DIM)),
        'cache_d_v': nrm(ks[5], (N_CD_LAYERS, DEC_BATCH, PAST_LEN, D_HEADS, D_HEAD_DIM)),
        'p_prompt': nrm(ks[6], (DEPTH, BATCH, SEQ, PLE_DIM)),
        'p_sample': nrm(ks[7], (DEPTH, DEC_BATCH, DEC_SEQ, PLE_DIM)),
        'norm_g': 1.0 + nrm(ks[8], (DEPTH, D_MODEL), 0.02),
        'ple_gate': nrm(ks[9], (DEPTH, D_MODEL, D_MODEL), D_MODEL ** -0.5),
        'ple_proj': nrm(ks[10], (DEPTH, PLE_DIM, D_MODEL), PLE_DIM ** -0.5),
        'ab_w_in': nrm(ks[11], (N_AB_LAYERS, D_MODEL, AB_IN), D_MODEL ** -0.5),
        'a_conv_w': nrm(ks[12], (N_AB_LAYERS, CONV_A, D_BRANCH), CONV_A ** -0.5),
        'b_conv_w': nrm(ks[13], (N_AB_LAYERS, CONV_B, D_BRANCH), CONV_B ** -0.5),
        'b_ln_g': 1.0 + nrm(ks[14], (N_AB_LAYERS, D_BRANCH), 0.02),
        'b_ln_b': nrm(ks[15], (N_AB_LAYERS, D_BRANCH), 0.02),
        'ab_w_out': nrm(ks[16], (N_AB_LAYERS, 2 * D_BRANCH, D_MODEL), (2 * D_BRANCH) ** -0.5),
        'cd_w_in': nrm(ks[17], (N_CD_LAYERS, D_MODEL, CD_IN), D_MODEL ** -0.5),
        'c_ln_g': 1.0 + nrm(ks[18], (N_CD_LAYERS, D_BRANCH), 0.02),
        'c_ln_b': nrm(ks[19], (N_CD_LAYERS, D_BRANCH), 0.02),
        'c_ws': nrm(ks[20], (N_CD_LAYERS, C_HEADS, C_CHUNK, C_CHUNK), C_CHUNK ** -0.5),
        'c_b': 1.0 + nrm(ks[21], (N_CD_LAYERS, C_HEADS, C_CHUNK), 0.1),
        'cd_w_out': nrm(ks[22], (N_CD_LAYERS, 2 * D_BRANCH, D_MODEL), (2 * D_BRANCH) ** -0.5),
        'final_g': 1.0 + nrm(ks[23], (D_MODEL,), 0.02),
    }


def reference(x_prompt, x_sample, state_a_conv, state_b_conv, cache_d_k, cache_d_v,
              p_prompt, p_sample, norm_g, ple_gate, ple_proj, ab_w_in, a_conv_w, b_conv_w,
              b_ln_g, b_ln_b, ab_w_out, cd_w_in, c_ln_g, c_ln_b, c_ws, c_b, cd_w_out, final_g):
    bp = x_prompt.shape[0]
    a_hist0 = jnp.zeros((N_AB_LAYERS, bp, CONV_A - 1, D_BRANCH), x_prompt.dtype)
    b_hist0 = jnp.zeros((N_AB_LAYERS, bp, CONV_B - 1, D_BRANCH), x_prompt.dtype)
    y_prompt, pa, pb, _, pk, pv = _trunk(
        x_prompt, p_prompt, a_hist0, b_hist0, None, None, 0, norm_g, ple_gate, ple_proj,
        ab_w_in, a_conv_w, b_conv_w, b_ln_g, b_ln_b, ab_w_out,
        cd_w_in, c_ln_g, c_ln_b, c_ws, c_b, cd_w_out, final_g)
    y_sample, sa, sb, scv, sk, sv = _trunk(
        x_sample, p_sample, state_a_conv, state_b_conv, cache_d_k, cache_d_v, PAST_LEN,
        norm_g, ple_gate, ple_proj, ab_w_in, a_conv_w, b_conv_w, b_ln_g, b_ln_b, ab_w_out,
        cd_w_in, c_ln_g, c_ln_b, c_ws, c_b, cd_w_out, final_g)
    return (y_prompt, y_sample, jnp.stack(pa), jnp.stack(sa), jnp.stack(pb), jnp.stack(sb),
            jnp.stack(scv), jnp.stack(pk), jnp.stack(pv), jnp.stack(sk), jnp.stack(sv))
```

```python
import functools

import jax
import jax.numpy as jnp
from jax import lax
from jax.experimental import pallas as pl
from jax.experimental.pallas import tpu as pltpu

D_MODEL = 1024
PLE_DIM = 256
D_BRANCH = 512
CONV_A = 3
CONV_B = 31
C_HEADS = 4
C_CHUNK = 128
C_HEAD_DIM = D_BRANCH // C_HEADS
D_HEADS = 8
D_HEAD_DIM = D_BRANCH // D_HEADS
EPS = 1e-6

LANES = 128
SUBLANES = 8
VMEM_LIMIT_BYTES = 56 * 1024 * 1024

HIST_A = CONV_A - 1
HIST_B = CONV_B - 1
PAD_A = SUBLANES
PAD_B = 4 * SUBLANES
CONV_ROWS = 32
KEY_BLOCK = 128
STICK_LOG_FLOOR = -88.0

F32 = jnp.float32
BF16 = jnp.bfloat16


def _rms(x, g):
    return x * lax.rsqrt(jnp.mean(x * x, axis=-1, keepdims=True) + EPS) * g


def _layernorm(x, g, b):
    mu = jnp.mean(x, axis=-1, keepdims=True)
    xc = x - mu
    return xc * lax.rsqrt(jnp.mean(xc * xc, axis=-1, keepdims=True) + EPS) * g + b


def _silu(x):
    return x * jax.nn.sigmoid(x)


def _dot(a, b):
    return jnp.dot(a, b, preferred_element_type=F32)


def _ab_layer_kernel(x_ref, p_ref, ha_ref, hb_ref, g_ref, win_ref, aw_ref, bw_ref,
                     lng_ref, lnb_ref, wout_ref, gate_ref, pproj_ref,
                     h_ref, na_ref, nb_ref,
                     ua_s, ub_s, ga_s, gb_s, ab_s, *, bb, tt):
    s = pl.program_id(1)
    m = bb * tt

    @pl.when(s == 0)
    def _load_history():
        ua_s[:, PAD_A - HIST_A:PAD_A, :] = ha_ref[...]
        ub_s[:, PAD_B - HIST_B:PAD_B, :] = hb_ref[...]

    x = x_ref[...].reshape(m, D_MODEL)
    xb = _rms(x, g_ref[...]).astype(BF16)

    def proj(i):
        return _dot(xb, win_ref[:, i * D_BRANCH:(i + 1) * D_BRANCH])

    ua_s[:, PAD_A:, :] = (proj(1) * proj(0)).reshape(bb, tt, D_BRANCH)
    ga_s[...] = proj(2) * _silu(proj(3))
    ub_s[:, PAD_B:, :] = (proj(4) * jax.nn.sigmoid(proj(5))).reshape(bb, tt, D_BRANCH)
    gb_s[...] = _silu(proj(6))

    for b in range(bb):
        for r0 in range(0, tt, CONV_ROWS):
            rows = pl.ds(b * tt + r0, CONV_ROWS)
            conv_a = aw_ref[0:1, :] * ua_s[b, pl.ds(PAD_A - HIST_A + r0, CONV_ROWS), :]
            for j in range(1, CONV_A):
                conv_a += aw_ref[j:j + 1, :] * ua_s[b, pl.ds(PAD_A - HIST_A + r0 + j, CONV_ROWS), :]
            ab_s[rows, 0:D_BRANCH] = (ga_s[rows, :] * conv_a).astype(BF16)

            conv_b = bw_ref[0:1, :] * ub_s[b, pl.ds(PAD_B - HIST_B + r0, CONV_ROWS), :]
            for j in range(1, CONV_B):
                conv_b += bw_ref[j:j + 1, :] * ub_s[b, pl.ds(PAD_B - HIST_B + r0 + j, CONV_ROWS), :]
            b_act = _silu(_layernorm(conv_b, lng_ref[...], lnb_ref[...]))
            ab_s[rows, D_BRANCH:2 * D_BRANCH] = (b_act * gb_s[rows, :]).astype(BF16)

    h = x + _dot(ab_s[...], wout_ref[...])
    gate = jax.nn.sigmoid(_dot(h.astype(BF16), gate_ref[...]))
    emb = _dot(p_ref[...].reshape(m, PLE_DIM).astype(BF16), pproj_ref[...])
    h_ref[...] = (h + gate * emb).reshape(bb, tt, D_MODEL)

    tail_a = ua_s[:, PAD_A + tt - HIST_A:PAD_A + tt, :]
    tail_b = ub_s[:, PAD_B + tt - HIST_B:PAD_B + tt, :]
    ua_s[:, PAD_A - HIST_A:PAD_A, :] = tail_a
    ub_s[:, PAD_B - HIST_B:PAD_B, :] = tail_b

    @pl.when(s == pl.num_programs(1) - 1)
    def _store_state():
        na_ref[...] = tail_a
        nb_ref[...] = tail_b


def _const_spec(shape):
    return pl.BlockSpec(shape, lambda b, s: (0,) * len(shape))


def _ab_layer(x, p_all, layer, hist_a, hist_b, g, w_in, a_w, b_w, ln_g, ln_b, w_out, gate, pproj,
              *, bb, tt):
    nb, t, _ = x.shape
    assert nb % bb == 0 and t % tt == 0 and tt % CONV_ROWS == 0 and tt >= HIST_B
    m = bb * tt
    grid = (nb // bb, t // tt)
    tok = lambda d: pl.BlockSpec((bb, tt, d), lambda b, s: (b, s, 0))
    state = lambda r: pl.BlockSpec((bb, r, D_BRANCH), lambda b, s: (b, 0, 0))
    return pl.pallas_call(
        functools.partial(_ab_layer_kernel, bb=bb, tt=tt),
        grid=grid,
        in_specs=[
            tok(D_MODEL),
            pl.BlockSpec((None, bb, tt, PLE_DIM), lambda b, s: (layer, b, s, 0)),
            state(HIST_A), state(HIST_B),
            _const_spec((1, D_MODEL)),
            _const_spec((D_MODEL, 7 * D_BRANCH)),
            _const_spec((CONV_A, D_BRANCH)),
            _const_spec((CONV_B, D_BRANCH)),
            _const_spec((1, D_BRANCH)), _const_spec((1, D_BRANCH)),
            _const_spec((2 * D_BRANCH, D_MODEL)),
            _const_spec((D_MODEL, D_MODEL)),
            _const_spec((PLE_DIM, D_MODEL)),
        ],
        out_specs=[tok(D_MODEL), state(HIST_A), state(HIST_B)],
        out_shape=[
            jax.ShapeDtypeStruct((nb, t, D_MODEL), F32),
            jax.ShapeDtypeStruct((nb, HIST_A, D_BRANCH), F32),
            jax.ShapeDtypeStruct((nb, HIST_B, D_BRANCH), F32),
        ],
        scratch_shapes=[
            pltpu.VMEM((bb, PAD_A + tt, D_BRANCH), F32),
            pltpu.VMEM((bb, PAD_B + tt, D_BRANCH), F32),
            pltpu.VMEM((m, D_BRANCH), F32),
            pltpu.VMEM((m, D_BRANCH), F32),
            pltpu.VMEM((m, 2 * D_BRANCH), BF16),
        ],
        compiler_params=pltpu.CompilerParams(
            dimension_semantics=("arbitrary", "arbitrary"),
            vmem_limit_bytes=VMEM_LIMIT_BYTES),
        name="ab_layer",
    )(x, p_all, hist_a, hist_b, g, w_in, a_w, b_w, ln_g, ln_b, w_out, gate, pproj)


def _cd_proj_kernel(h_ref, g_ref, win_ref, lng_ref, lnb_ref, ws_ref, cbt_ref,
                    cout_ref, q_ref, k_ref, v_ref, kb_ref, vb_ref, sdz_ref, cvn_ref,
                    cvn_s, gc_s, *, bb, tt, chunk):
    m = bb * tt
    x = h_ref[...].reshape(m, D_MODEL)
    xb = _rms(x, g_ref[...]).astype(BF16)

    def proj(i):
        return _dot(xb, win_ref[:, i * D_BRANCH:(i + 1) * D_BRANCH])

    gc_s[...] = proj(0) * _silu(proj(2))
    cvn = _layernorm(proj(1), lng_ref[...], lnb_ref[...])
    cvn_ref[...] = cvn.reshape(bb, tt, D_BRANCH)
    cvn_s[...] = cvn.astype(BF16)
    q_ref[...] = (proj(3) * (D_HEAD_DIM ** -0.5)).astype(BF16).reshape(bb, tt, D_BRANCH)
    k = proj(4).reshape(bb, tt, D_BRANCH)
    k_ref[...] = k
    kb_ref[...] = k.astype(BF16)
    v = proj(5).reshape(bb, tt, D_BRANCH)
    v_ref[...] = v
    vb_ref[...] = v.astype(BF16)
    sdz_ref[...] = _silu(proj(6)).reshape(bb, tt, D_BRANCH)

    row = lax.broadcasted_iota(jnp.int32, (chunk, chunk), 0)
    col = lax.broadcasted_iota(jnp.int32, (chunk, chunk), 1)
    for hd in range(C_HEADS):
        lanes = slice(hd * C_HEAD_DIM, (hd + 1) * C_HEAD_DIM)
        w_tril = jnp.where(col <= row, ws_ref[hd], 0.0).astype(BF16)
        bias = cbt_ref[:, hd:hd + 1]
        for c0 in range(0, m, chunk):
            rows = pl.ds(c0, chunk)
            mixed = _dot(w_tril, cvn_s[rows, lanes]) + bias
            cout_ref[c0 // tt, pl.ds(c0 % tt, chunk), lanes] = (gc_s[rows, lanes] * mixed).astype(BF16)


def _cd_proj(h, g, w_in, ln_g, ln_b, ws, cbt, *, bb, tt, chunk):
    nb, t, _ = h.shape
    assert nb % bb == 0 and t % tt == 0 and tt % chunk == 0
    m = bb * tt
    grid = (nb // bb, t // tt)
    tok = lambda d: pl.BlockSpec((bb, tt, d), lambda b, s: (b, s, 0))
    shp = lambda dt: jax.ShapeDtypeStruct((nb, t, D_BRANCH), dt)
    return pl.pallas_call(
        functools.partial(_cd_proj_kernel, bb=bb, tt=tt, chunk=chunk),
        grid=grid,
        in_specs=[
            tok(D_MODEL),
            _const_spec((1, D_MODEL)),
            _const_spec((D_MODEL, 7 * D_BRANCH)),
            _const_spec((1, D_BRANCH)), _const_spec((1, D_BRANCH)),
            _const_spec((C_HEADS, chunk, chunk)),
            _const_spec((chunk, C_HEADS)),
        ],
        out_specs=[tok(D_BRANCH)] * 8,
        out_shape=[shp(BF16), shp(BF16), shp(F32), shp(F32), shp(BF16), shp(BF16), shp(F32), shp(F32)],
        scratch_shapes=[
            pltpu.VMEM((m, D_BRANCH), BF16),
            pltpu.VMEM((m, D_BRANCH), F32),
        ],
        compiler_params=pltpu.CompilerParams(
            dimension_semantics=("arbitrary", "arbitrary"),
            vmem_limit_bytes=VMEM_LIMIT_BYTES),
        name="cd_proj",
    )(h, g, w_in, ln_g, ln_b, ws, cbt)


def _sb_setup(q, qm_s, acc_s, car_s, tri_s):
    tq = q.shape[0]
    lane = lax.broadcasted_iota(jnp.int32, (tq, LANES), 1)
    for hd in range(D_HEADS):
        pair = q[:, (hd // 2) * LANES:(hd // 2 + 1) * LANES]
        in_head = (lane >= D_HEAD_DIM) if hd % 2 else (lane < D_HEAD_DIM)
        qm_s[hd] = jnp.where(in_head, pair, jnp.zeros_like(pair))
    acc_s[...] = jnp.zeros_like(acc_s)
    car_s[...] = jnp.zeros_like(car_s)
    row = lax.broadcasted_iota(jnp.int32, (KEY_BLOCK, 2 * KEY_BLOCK), 0)
    col = lax.broadcasted_iota(jnp.int32, (KEY_BLOCK, 2 * KEY_BLOCK), 1)
    tri_s[...] = jnp.where((row > col) | (col >= KEY_BLOCK), 1.0, 0.0).astype(BF16)


def _sb_block(kblk, vblk, qm_s, acc_s, car_s, tri_s, *, diagonal):
    tq = qm_s.shape[1]
    if diagonal:
        qi = lax.broadcasted_iota(jnp.int32, (tq, KEY_BLOCK), 0)
        ki = lax.broadcasted_iota(jnp.int32, (tq, KEY_BLOCK), 1)
        valid = ki < qi
    tri = tri_s[...]
    for hd in range(D_HEADS):
        lanes = slice((hd // 2) * LANES, (hd // 2 + 1) * LANES)
        z = lax.dot_general(qm_s[hd], kblk[:, lanes], (((1,), (1,)), ((), ())),
                            preferred_element_type=F32)
        soft = jnp.log(1.0 + jnp.exp(-jnp.abs(z)))
        log_beta = jnp.minimum(z, 0.0) - soft
        log_keep = log_beta - z
        if diagonal:
            log_keep = jnp.where(valid, log_keep, 0.0)
        hi = log_keep.astype(BF16)
        lo = (log_keep - hi.astype(F32)).astype(BF16)
        sums = _dot(hi, tri) + _dot(lo, tri)
        later = sums[:, :KEY_BLOCK]
        total = sums[:, KEY_BLOCK:]
        a = jnp.exp(log_beta + later + car_s[hd])
        if diagonal:
            a = jnp.where(valid, a, 0.0)
        acc_s[hd] += _dot(a.astype(BF16), vblk[:, lanes])
        car_s[hd] += total


def _sb_all_spent(car_s):
    worst = car_s[0]
    for hd in range(1, D_HEADS):
        worst = jnp.maximum(worst, car_s[hd])
    return (jnp.max(worst) < STICK_LOG_FLOOR).astype(jnp.int32)


def _sb_sweep(first_block, load_block, qm_s, acc_s, car_s, tri_s):
    def cond(c):
        j, spent = c
        return jnp.logical_and(j >= 0, spent == 0)

    def body(c):
        j, _ = c
        kblk, vblk = load_block(j)
        _sb_block(kblk, vblk, qm_s, acc_s, car_s, tri_s, diagonal=False)
        return j - 1, _sb_all_spent(car_s)

    lax.while_loop(cond, body, (jnp.asarray(first_block, jnp.int32), _sb_all_spent(car_s)))


def _sb_finish(o_ref, acc_s):
    tq = acc_s.shape[1]
    lane = lax.broadcasted_iota(jnp.int32, (tq, LANES), 1)
    for pr in range(D_HEADS // 2):
        o_ref[0, :, pr * LANES:(pr + 1) * LANES] = jnp.where(
            lane < D_HEAD_DIM, acc_s[2 * pr], acc_s[2 * pr + 1])


def _sb_self_kernel(q_ref, k_ref, v_ref, o_ref, qm_s, acc_s, car_s, tri_s):
    i = pl.program_id(1)
    _sb_setup(q_ref[0], qm_s, acc_s, car_s, tri_s)

    def load_block(j):
        rows = pl.ds(pl.multiple_of(j * KEY_BLOCK, KEY_BLOCK), KEY_BLOCK)
        return k_ref[0, rows, :], v_ref[0, rows, :]

    kd, vd = load_block(i)
    _sb_block(kd, vd, qm_s, acc_s, car_s, tri_s, diagonal=True)
    _sb_sweep(i - 1, load_block, qm_s, acc_s, car_s, tri_s)
    _sb_finish(o_ref, acc_s)


def _sb_cached_kernel(q_ref, kn_ref, vn_ref, kp_ref, vp_ref, o_ref,
                      qm_s, acc_s, car_s, tri_s, kd_s, vd_s, *, n_past_blocks):
    tn = kn_ref.shape[1]
    _sb_setup(q_ref[0], qm_s, acc_s, car_s, tri_s)
    kd_s[...] = jnp.zeros_like(kd_s)
    vd_s[...] = jnp.zeros_like(vd_s)
    kd_s[0:tn, :] = kn_ref[0]
    vd_s[0:tn, :] = vn_ref[0]
    _sb_block(kd_s[...], vd_s[...], qm_s, acc_s, car_s, tri_s, diagonal=True)

    def load_block(j):
        rows = pl.ds(pl.multiple_of(j * KEY_BLOCK, KEY_BLOCK), KEY_BLOCK)
        return kp_ref[0, rows, :].astype(BF16), vp_ref[0, rows, :].astype(BF16)

    _sb_sweep(n_past_blocks - 1, load_block, qm_s, acc_s, car_s, tri_s)
    _sb_finish(o_ref, acc_s)


def _sb_scratch(tq):
    return [
        pltpu.VMEM((D_HEADS, tq, LANES), BF16),
        pltpu.VMEM((D_HEADS, tq, LANES), F32),
        pltpu.VMEM((D_HEADS, tq, LANES), F32),
        pltpu.VMEM((KEY_BLOCK, 2 * KEY_BLOCK), BF16),
    ]


def _sb_self(q, kb, vb):
    nb, t, _ = q.shape
    assert t % KEY_BLOCK == 0
    full = pl.BlockSpec((1, t, D_BRANCH), lambda b, i: (b, 0, 0))
    blk = pl.BlockSpec((1, KEY_BLOCK, D_BRANCH), lambda b, i: (b, i, 0))
    return pl.pallas_call(
        _sb_self_kernel,
        grid=(nb, t // KEY_BLOCK),
        in_specs=[blk, full, full],
        out_specs=blk,
        out_shape=jax.ShapeDtypeStruct((nb, t, D_BRANCH), F32),
        scratch_shapes=_sb_scratch(KEY_BLOCK),
        compiler_params=pltpu.CompilerParams(
            dimension_semantics=("arbitrary", "arbitrary"),
            vmem_limit_bytes=VMEM_LIMIT_BYTES),
        name="sb_self",
    )(q, kb, vb)


def _sb_cached(q, kb_new, vb_new, k_past, v_past):
    nb, tq, _ = q.shape
    t_past = k_past.shape[1]
    assert tq <= KEY_BLOCK and tq % (2 * SUBLANES) == 0 and t_past % KEY_BLOCK == 0
    new = pl.BlockSpec((1, tq, D_BRANCH), lambda b: (b, 0, 0))
    past = pl.BlockSpec((1, t_past, D_BRANCH), lambda b: (b, 0, 0))
    return pl.pallas_call(
        functools.partial(_sb_cached_kernel, n_past_blocks=t_past // KEY_BLOCK),
        grid=(nb,),
        in_specs=[new, new, new, past, past],
        out_specs=new,
        out_shape=jax.ShapeDtypeStruct((nb, tq, D_BRANCH), F32),
        scratch_shapes=_sb_scratch(tq) + [
            pltpu.VMEM((KEY_BLOCK, D_BRANCH), BF16),
            pltpu.VMEM((KEY_BLOCK, D_BRANCH), BF16),
        ],
        compiler_params=pltpu.CompilerParams(
            dimension_semantics=("arbitrary",),
            vmem_limit_bytes=VMEM_LIMIT_BYTES),
        name="sb_cached",
    )(q, kb_new, vb_new, k_past, v_past)


def _cd_out_kernel(h_ref, cout_ref, o_ref, sdz_ref, p_ref, wout_ref, gate_ref, pproj_ref, fg_ref,
                   y_ref, *, bb, tt):
    m = bb * tt
    x = h_ref[...].reshape(m, D_MODEL)
    d_out = (o_ref[...] * sdz_ref[...]).reshape(m, D_BRANCH).astype(BF16)
    c_out = cout_ref[...].reshape(m, D_BRANCH)
    h = x + _dot(c_out, wout_ref[0:D_BRANCH, :]) + _dot(d_out, wout_ref[D_BRANCH:, :])
    gate = jax.nn.sigmoid(_dot(h.astype(BF16), gate_ref[...]))
    emb = _dot(p_ref[...].reshape(m, PLE_DIM).astype(BF16), pproj_ref[...])
    y_ref[...] = _rms(h + gate * emb, fg_ref[...]).reshape(bb, tt, D_MODEL)


def _cd_out(h, cout, o, sdz, p_all, layer, w_out, gate, pproj, fg, *, bb, tt):
    nb, t, _ = h.shape
    grid = (nb // bb, t // tt)
    tok = lambda d: pl.BlockSpec((bb, tt, d), lambda b, s: (b, s, 0))
    return pl.pallas_call(
        functools.partial(_cd_out_kernel, bb=bb, tt=tt),
        grid=grid,
        in_specs=[
            tok(D_MODEL), tok(D_BRANCH), tok(D_BRANCH), tok(D_BRANCH),
            pl.BlockSpec((None, bb, tt, PLE_DIM), lambda b, s: (layer, b, s, 0)),
            _const_spec((2 * D_BRANCH, D_MODEL)),
            _const_spec((D_MODEL, D_MODEL)),
            _const_spec((PLE_DIM, D_MODEL)),
            _const_spec((1, D_MODEL)),
        ],
        out_specs=tok(D_MODEL),
        out_shape=jax.ShapeDtypeStruct((nb, t, D_MODEL), F32),
        compiler_params=pltpu.CompilerParams(
            dimension_semantics=("arbitrary", "arbitrary"),
            vmem_limit_bytes=VMEM_LIMIT_BYTES),
        name="cd_out",
    )(h, cout, o, sdz, p_all, w_out, gate, pproj, fg)


def _trunk(x, p_all, hist_a, hist_b, k_past, v_past, w, *, bb, tt):
    t = x.shape[1]
    chunk = min(t, C_CHUNK)
    h1, new_a, new_b = _ab_layer(
        x, p_all, 0, hist_a, hist_b, w["norm_g"][0], w["ab_w_in"], w["a_conv_w"], w["b_conv_w"],
        w["b_ln_g"], w["b_ln_b"], w["ab_w_out"], w["ple_gate"][0], w["ple_proj"][0], bb=bb, tt=tt)
    cout, q, k, v, kb, vb, sdz, cvn = _cd_proj(
        h1, w["norm_g"][1], w["cd_w_in"], w["c_ln_g"], w["c_ln_b"],
        w["c_ws"][:, :chunk, :chunk], w["c_bt"][:chunk], bb=bb, tt=tt, chunk=chunk)
    if k_past is None:
        o = _sb_self(q, kb, vb)
    else:
        o = _sb_cached(q, kb, vb, k_past, v_past)
    y = _cd_out(h1, cout, o, sdz, p_all, 1, w["cd_w_out"], w["ple_gate"][1], w["ple_proj"][1],
                w["final_g"], bb=bb, tt=tt)
    return y, new_a, new_b, cvn, k, v


def kernel(x_prompt, x_sample, state_a_conv, state_b_conv, cache_d_k, cache_d_v, p_prompt, p_sample,
           norm_g, ple_gate, ple_proj, ab_w_in, a_conv_w, b_conv_w, b_ln_g, b_ln_b, ab_w_out,
           cd_w_in, c_ln_g, c_ln_b, c_ws, c_b, cd_w_out, final_g):
    bp, tp, _ = x_prompt.shape
    bs, ts, _ = x_sample.shape
    w = {
        "norm_g": norm_g.reshape(2, 1, D_MODEL),
        "ple_gate": ple_gate.astype(BF16),
        "ple_proj": ple_proj.astype(BF16),
        "ab_w_in": ab_w_in[0].astype(BF16),
        "a_conv_w": a_conv_w[0],
        "b_conv_w": b_conv_w[0],
        "b_ln_g": b_ln_g, "b_ln_b": b_ln_b,
        "ab_w_out": ab_w_out[0].astype(BF16),
        "cd_w_in": cd_w_in[0].astype(BF16),
        "c_ln_g": c_ln_g, "c_ln_b": c_ln_b,
        "c_ws": c_ws[0],
        "c_bt": c_b[0].T,
        "cd_w_out": cd_w_out[0].astype(BF16),
        "final_g": final_g.reshape(1, D_MODEL),
    }
    zeros_a = jnp.zeros((bp, HIST_A, D_BRANCH), F32)
    zeros_b = jnp.zeros((bp, HIST_B, D_BRANCH), F32)
    yp, pa, pb, _, pk, pv = _trunk(x_prompt, p_prompt, zeros_a, zeros_b, None, None, w, bb=1, tt=256)
    k_past = cache_d_k[0].reshape(bs, -1, D_BRANCH)
    v_past = cache_d_v[0].reshape(bs, -1, D_BRANCH)
    ys, sa, sb, scv, sk, sv = _trunk(x_sample, p_sample, state_a_conv[0], state_b_conv[0],
                                     k_past, v_past, w, bb=8, tt=ts)
    heads = lambda a: a.reshape(1, a.shape[0], a.shape[1], D_HEADS, D_HEAD_DIM)
    return (yp, ys, pa[None], sa[None], pb[None], sb[None], scv[None],
            heads(pk), heads(pv), heads(sk), heads(sv))
```

```python
import functools

import jax
import jax.numpy as jnp
from jax import lax
from jax.experimental import pallas as pl
from jax.experimental.pallas import tpu as pltpu

D_MODEL = 1024
PLE_DIM = 256
D_BRANCH = 512
CONV_A = 3
CONV_B = 31
C_HEADS = 4
C_CHUNK = 128
C_HEAD_DIM = D_BRANCH // C_HEADS
D_HEADS = 8
D_HEAD_DIM = D_BRANCH // D_HEADS
EPS = 1e-6

LANES = 128
SUBLANES = 8
VMEM_LIMIT_BYTES = 56 * 1024 * 1024

HIST_A = CONV_A - 1
HIST_B = CONV_B - 1
PAD_A = SUBLANES
PAD_B = 4 * SUBLANES
CONV_ROWS = 32
KEY_BLOCK = 128
STICK_LOG_FLOOR = -88.0

F32 = jnp.float32
BF16 = jnp.bfloat16


def _rms(x, g):
    return x * lax.rsqrt(jnp.mean(x * x, axis=-1, keepdims=True) + EPS) * g


def _layernorm(x, g, b):
    mu = jnp.mean(x, axis=-1, keepdims=True)
    xc = x - mu
    return xc * lax.rsqrt(jnp.mean(xc * xc, axis=-1, keepdims=True) + EPS) * g + b


def _silu(x):
    return x * jax.nn.sigmoid(x)


def _dot(a, b):
    return jnp.dot(a, b, preferred_element_type=F32)


def _ab_layer_kernel(x_ref, p_ref, ha_ref, hb_ref, g_ref, win_ref, aw_ref, bw_ref,
                     lng_ref, lnb_ref, wout_ref, gate_ref, pproj_ref,
                     h_ref, na_ref, nb_ref,
                     ua_s, ub_s, sh_s, ga_s, gb_s, ab_s, *, bb, tt):
    s = pl.program_id(1)
    m = bb * tt

    @pl.when(s == 0)
    def _load_history():
        ua_s[:, PAD_A - HIST_A:PAD_A, :] = ha_ref[...]
        ub_s[:, PAD_B - HIST_B:PAD_B, :] = hb_ref[...]

    x = x_ref[...].reshape(m, D_MODEL)
    xb = _rms(x, g_ref[...]).astype(BF16)

    def proj(i):
        return _dot(xb, win_ref[:, i * D_BRANCH:(i + 1) * D_BRANCH])

    ua_s[:, PAD_A:, :] = (proj(1) * proj(0)).reshape(bb, tt, D_BRANCH)
    ga_s[...] = proj(2) * _silu(proj(3))
    ub_s[:, PAD_B:, :] = (proj(4) * jax.nn.sigmoid(proj(5))).reshape(bb, tt, D_BRANCH)
    gb_s[...] = _silu(proj(6))

    for b in range(bb):
        for shift in range(1, SUBLANES):
            sh_s[shift - 1] = ub_s[b, shift:shift + PAD_B + tt - SUBLANES, :]
        for r0 in range(0, tt, CONV_ROWS):
            rows = pl.ds(b * tt + r0, CONV_ROWS)
            conv_a = aw_ref[0:1, :] * ua_s[b, pl.ds(PAD_A - HIST_A + r0, CONV_ROWS), :]
            for j in range(1, CONV_A):
                conv_a += aw_ref[j:j + 1, :] * ua_s[b, pl.ds(PAD_A - HIST_A + r0 + j, CONV_ROWS), :]
            ab_s[rows, 0:D_BRANCH] = (ga_s[rows, :] * conv_a).astype(BF16)

            conv_b = None
            for j in range(CONV_B):
                blk, shift = divmod(PAD_B - HIST_B + j, SUBLANES)
                src = pl.ds(r0 + blk * SUBLANES, CONV_ROWS)
                tap = ub_s[b, src, :] if shift == 0 else sh_s[shift - 1, src, :]
                term = bw_ref[j:j + 1, :] * tap
                conv_b = term if conv_b is None else conv_b + term
            b_act = _silu(_layernorm(conv_b, lng_ref[...], lnb_ref[...]))
            ab_s[rows, D_BRANCH:2 * D_BRANCH] = (b_act * gb_s[rows, :]).astype(BF16)

    h = x + _dot(ab_s[...], wout_ref[...])
    gate = jax.nn.sigmoid(_dot(h.astype(BF16), gate_ref[...]))
    emb = _dot(p_ref[...].reshape(m, PLE_DIM).astype(BF16), pproj_ref[...])
    h_ref[...] = (h + gate * emb).reshape(bb, tt, D_MODEL)

    tail_a = ua_s[:, PAD_A + tt - HIST_A:PAD_A + tt, :]
    tail_b = ub_s[:, PAD_B + tt - HIST_B:PAD_B + tt, :]
    ua_s[:, PAD_A - HIST_A:PAD_A, :] = tail_a
    ub_s[:, PAD_B - HIST_B:PAD_B, :] = tail_b

    @pl.when(s == pl.num_programs(1) - 1)
    def _store_state():
        na_ref[...] = tail_a
        nb_ref[...] = tail_b


def _const_spec(shape):
    return pl.BlockSpec(shape, lambda b, s: (0,) * len(shape))


def _ab_layer(x, p_all, layer, hist_a, hist_b, g, w_in, a_w, b_w, ln_g, ln_b, w_out, gate, pproj,
              *, bb, tt):
    nb, t, _ = x.shape
    assert nb % bb == 0 and t % tt == 0 and tt % CONV_ROWS == 0 and tt >= HIST_B
    m = bb * tt
    grid = (nb // bb, t // tt)
    tok = lambda d: pl.BlockSpec((bb, tt, d), lambda b, s: (b, s, 0))
    state = lambda r: pl.BlockSpec((bb, r, D_BRANCH), lambda b, s: (b, 0, 0))
    return pl.pallas_call(
        functools.partial(_ab_layer_kernel, bb=bb, tt=tt),
        grid=grid,
        in_specs=[
            tok(D_MODEL),
            pl.BlockSpec((None, bb, tt, PLE_DIM), lambda b, s: (layer, b, s, 0)),
            state(HIST_A), state(HIST_B),
            _const_spec((1, D_MODEL)),
            _const_spec((D_MODEL, 7 * D_BRANCH)),
            _const_spec((CONV_A, D_BRANCH)),
            _const_spec((CONV_B, D_BRANCH)),
            _const_spec((1, D_BRANCH)), _const_spec((1, D_BRANCH)),
            _const_spec((2 * D_BRANCH, D_MODEL)),
            _const_spec((D_MODEL, D_MODEL)),
            _const_spec((PLE_DIM, D_MODEL)),
        ],
        out_specs=[tok(D_MODEL), state(HIST_A), state(HIST_B)],
        out_shape=[
            jax.ShapeDtypeStruct((nb, t, D_MODEL), F32),
            jax.ShapeDtypeStruct((nb, HIST_A, D_BRANCH), F32),
            jax.ShapeDtypeStruct((nb, HIST_B, D_BRANCH), F32),
        ],
        scratch_shapes=[
            pltpu.VMEM((bb, PAD_A + tt, D_BRANCH), F32),
            pltpu.VMEM((bb, PAD_B + tt, D_BRANCH), F32),
            pltpu.VMEM((SUBLANES - 1, PAD_B + tt - SUBLANES, D_BRANCH), F32),
            pltpu.VMEM((m, D_BRANCH), F32),
            pltpu.VMEM((m, D_BRANCH), F32),
            pltpu.VMEM((m, 2 * D_BRANCH), BF16),
        ],
        compiler_params=pltpu.CompilerParams(
            dimension_semantics=("arbitrary", "arbitrary"),
            vmem_limit_bytes=VMEM_LIMIT_BYTES),
        name="ab_layer",
    )(x, p_all, hist_a, hist_b, g, w_in, a_w, b_w, ln_g, ln_b, w_out, gate, pproj)


def _cd_proj_kernel(h_ref, g_ref, win_ref, lng_ref, lnb_ref, ws_ref, cbt_ref,
                    cout_ref, q_ref, k_ref, v_ref, kb_ref, vb_ref, sdz_ref, cvn_ref,
                    cvn_s, gc_s, *, bb, tt, chunk):
    m = bb * tt
    x = h_ref[...].reshape(m, D_MODEL)
    xb = _rms(x, g_ref[...]).astype(BF16)

    def proj(i):
        return _dot(xb, win_ref[:, i * D_BRANCH:(i + 1) * D_BRANCH])

    gc_s[...] = proj(0) * _silu(proj(2))
    cvn = _layernorm(proj(1), lng_ref[...], lnb_ref[...])
    cvn_ref[...] = cvn.reshape(bb, tt, D_BRANCH)
    cvn_s[...] = cvn.astype(BF16)
    q_ref[...] = (proj(3) * (D_HEAD_DIM ** -0.5)).astype(BF16).reshape(bb, tt, D_BRANCH)
    k = proj(4).reshape(bb, tt, D_BRANCH)
    k_ref[...] = k
    kb_ref[...] = k.astype(BF16)
    v = proj(5).reshape(bb, tt, D_BRANCH)
    v_ref[...] = v
    vb_ref[...] = v.astype(BF16)
    sdz_ref[...] = _silu(proj(6)).reshape(bb, tt, D_BRANCH)

    row = lax.broadcasted_iota(jnp.int32, (chunk, chunk), 0)
    col = lax.broadcasted_iota(jnp.int32, (chunk, chunk), 1)
    for hd in range(C_HEADS):
        lanes = slice(hd * C_HEAD_DIM, (hd + 1) * C_HEAD_DIM)
        w_tril = jnp.where(col <= row, ws_ref[hd], 0.0).astype(BF16)
        bias = cbt_ref[:, hd:hd + 1]
        for c0 in range(0, m, chunk):
            rows = pl.ds(c0, chunk)
            mixed = _dot(w_tril, cvn_s[rows, lanes]) + bias
            cout_ref[c0 // tt, pl.ds(c0 % tt, chunk), lanes] = (gc_s[rows, lanes] * mixed).astype(BF16)


def _cd_proj(h, g, w_in, ln_g, ln_b, ws, cbt, *, bb, tt, chunk):
    nb, t, _ = h.shape
    assert nb % bb == 0 and t % tt == 0 and tt % chunk == 0
    m = bb * tt
    grid = (nb // bb, t // tt)
    tok = lambda d: pl.BlockSpec((bb, tt, d), lambda b, s: (b, s, 0))
    shp = lambda dt: jax.ShapeDtypeStruct((nb, t, D_BRANCH), dt)
    return pl.pallas_call(
        functools.partial(_cd_proj_kernel, bb=bb, tt=tt, chunk=chunk),
        grid=grid,
        in_specs=[
            tok(D_MODEL),
            _const_spec((1, D_MODEL)),
            _const_spec((D_MODEL, 7 * D_BRANCH)),
            _const_spec((1, D_BRANCH)), _const_spec((1, D_BRANCH)),
            _const_spec((C_HEADS, chunk, chunk)),
            _const_spec((chunk, C_HEADS)),
        ],
        out_specs=[tok(D_BRANCH)] * 8,
        out_shape=[shp(BF16), shp(BF16), shp(F32), shp(F32), shp(BF16), shp(BF16), shp(F32), shp(F32)],
        scratch_shapes=[
            pltpu.VMEM((m, D_BRANCH), BF16),
            pltpu.VMEM((m, D_BRANCH), F32),
        ],
        compiler_params=pltpu.CompilerParams(
            dimension_semantics=("arbitrary", "arbitrary"),
            vmem_limit_bytes=VMEM_LIMIT_BYTES),
        name="cd_proj",
    )(h, g, w_in, ln_g, ln_b, ws, cbt)


def _dot_nt(a, b):
    return lax.dot_general(a, b, (((1,), (1,)), ((), ())), preferred_element_type=F32)


def _pair_lanes(pr):
    return slice(pr * LANES, (pr + 1) * LANES)


def _sb_setup(q, qm_s, tri_s):
    tq = q.shape[0]
    lane = lax.broadcasted_iota(jnp.int32, (tq, LANES), 1)
    for hd in range(D_HEADS):
        pair = q[:, _pair_lanes(hd // 2)]
        in_head = (lane >= D_HEAD_DIM) if hd % 2 else (lane < D_HEAD_DIM)
        qm_s[hd * tq:(hd + 1) * tq, :] = jnp.where(in_head, pair, jnp.zeros_like(pair))
    n = tri_s.shape[0]
    row = lax.broadcasted_iota(jnp.int32, (n, n), 0)
    col = lax.broadcasted_iota(jnp.int32, (n, n), 1)
    tri_s[...] = jnp.where(row > col, 1.0, 0.0).astype(BF16)


def _sb_first_mask(tq, prev_ok):
    shape = (2 * tq, 2 * KEY_BLOCK)
    r = lax.broadcasted_iota(jnp.int32, shape, 0)
    qi = jnp.where(r >= tq, r - tq, r)
    c = lax.broadcasted_iota(jnp.int32, shape, 1)
    causal = (c - KEY_BLOCK) < qi
    if prev_ok is None:
        return causal
    return causal & ((c >= KEY_BLOCK) | (jnp.broadcast_to(prev_ok, shape) > 0))


def _sb_pass(qk, av, nk, valid, first, qm_s, acc_s, car_s, tri_s, lb_s, hi_s, lo_s, a_s):
    m2 = qm_s.shape[0] // (D_HEADS // 2)
    pairs = [pl.ds(pr * m2, m2) for pr in range(D_HEADS // 2)]
    tri = tri_s[0:nk, 0:nk]
    for pr, rows in enumerate(pairs):
        z = qk(pr, qm_s[rows, :])
        soft = jnp.log(1.0 + jnp.exp(-jnp.abs(z)))
        log_beta = jnp.minimum(z, 0.0) - soft
        log_keep = log_beta - z
        if valid is not None:
            log_keep = jnp.where(valid, log_keep, 0.0)
        hi = log_keep.astype(BF16)
        hi_s[rows, 0:nk] = hi
        lo_s[rows, 0:nk] = (log_keep - hi.astype(F32)).astype(BF16)
        total = jnp.sum(log_keep, axis=-1, keepdims=True)
        if first:
            lb_s[rows, 0:nk] = log_beta
            car_s[rows, :] = jnp.broadcast_to(total, (m2, LANES))
        else:
            carry = car_s[rows, :]
            lb_s[rows, 0:nk] = log_beta + jnp.concatenate([carry] * (nk // LANES), axis=1)
            car_s[rows, :] = carry + total
    for pr, rows in enumerate(pairs):
        later = _dot(hi_s[rows, 0:nk], tri) + _dot(lo_s[rows, 0:nk], tri)
        a = jnp.exp(lb_s[rows, 0:nk] + later)
        if valid is not None:
            a = jnp.where(valid, a, 0.0)
        a_s[rows, 0:nk] = a.astype(BF16)
    for pr, rows in enumerate(pairs):
        pv = av(pr, a_s[rows, 0:nk])
        acc_s[rows, :] = pv if first else acc_s[rows, :] + pv


def _sb_all_spent(car_s):
    return (jnp.max(car_s[...]) < STICK_LOG_FLOOR).astype(jnp.int32)


def _sb_sweep(first_block, tail_pass, car_s):
    def cond(c):
        j, spent = c
        return jnp.logical_and(j >= 0, spent == 0)

    def body(c):
        j, _ = c
        tail_pass(j)
        return j - 1, _sb_all_spent(car_s)

    lax.while_loop(cond, body, (jnp.asarray(first_block, jnp.int32), _sb_all_spent(car_s)))


def _sb_finish(o_ref, acc_s):
    tq = acc_s.shape[0] // D_HEADS
    lane = lax.broadcasted_iota(jnp.int32, (tq, LANES), 1)
    for pr in range(D_HEADS // 2):
        even = acc_s[(2 * pr) * tq:(2 * pr + 1) * tq, :]
        odd = acc_s[(2 * pr + 1) * tq:(2 * pr + 2) * tq, :]
        o_ref[0, :, _pair_lanes(pr)] = jnp.where(lane < D_HEAD_DIM, even, odd)


def _key_rows(j):
    return pl.ds(pl.multiple_of(j * KEY_BLOCK, KEY_BLOCK), KEY_BLOCK)


def _sb_self_kernel(q_ref, k_ref, v_ref, o_ref, qm_s, acc_s, car_s, tri_s, lb_s, hi_s, lo_s, a_s):
    i = pl.program_id(1)
    tq = q_ref.shape[1]
    scratch = (qm_s, acc_s, car_s, tri_s, lb_s, hi_s, lo_s, a_s)
    _sb_setup(q_ref[0], qm_s, tri_s)
    prev, diag = _key_rows(jnp.maximum(i - 1, 0)), _key_rows(i)

    def both(ref, pr):
        return jnp.concatenate([ref[0, prev, _pair_lanes(pr)], ref[0, diag, _pair_lanes(pr)]], axis=0)

    _sb_pass(lambda pr, q: _dot_nt(q, both(k_ref, pr)), lambda pr, a: _dot(a, both(v_ref, pr)),
             2 * KEY_BLOCK, _sb_first_mask(tq, i), True, *scratch)

    def tail_pass(j):
        rows = _key_rows(j)
        _sb_pass(lambda pr, q: _dot_nt(q, k_ref[0, rows, _pair_lanes(pr)]),
                 lambda pr, a: _dot(a, v_ref[0, rows, _pair_lanes(pr)]),
                 KEY_BLOCK, None, False, *scratch)

    _sb_sweep(i - 2, tail_pass, car_s)
    _sb_finish(o_ref, acc_s)


def _sb_cached_kernel(q_ref, kn_ref, vn_ref, kp_ref, vp_ref, o_ref,
                      qm_s, acc_s, car_s, tri_s, lb_s, hi_s, lo_s, a_s, kd_s, vd_s, *, n_past_blocks):
    tq = q_ref.shape[1]
    tn = kn_ref.shape[1]
    scratch = (qm_s, acc_s, car_s, tri_s, lb_s, hi_s, lo_s, a_s)
    _sb_setup(q_ref[0], qm_s, tri_s)
    last = _key_rows(n_past_blocks - 1)
    kd_s[0:KEY_BLOCK, :] = kp_ref[0, last, :].astype(BF16)
    vd_s[0:KEY_BLOCK, :] = vp_ref[0, last, :].astype(BF16)
    kd_s[KEY_BLOCK:, :] = jnp.zeros((KEY_BLOCK, D_BRANCH), BF16)
    vd_s[KEY_BLOCK:, :] = jnp.zeros((KEY_BLOCK, D_BRANCH), BF16)
    kd_s[KEY_BLOCK:KEY_BLOCK + tn, :] = kn_ref[0]
    vd_s[KEY_BLOCK:KEY_BLOCK + tn, :] = vn_ref[0]
    _sb_pass(lambda pr, q: _dot_nt(q, kd_s[:, _pair_lanes(pr)]), lambda pr, a: _dot(a, vd_s[:, _pair_lanes(pr)]),
             2 * KEY_BLOCK, _sb_first_mask(tq, None), True, *scratch)

    def tail_pass(j):
        rows = _key_rows(j)
        _sb_pass(lambda pr, q: _dot_nt(q, kp_ref[0, rows, _pair_lanes(pr)].astype(BF16)),
                 lambda pr, a: _dot(a, vp_ref[0, rows, _pair_lanes(pr)].astype(BF16)),
                 KEY_BLOCK, None, False, *scratch)

    _sb_sweep(n_past_blocks - 2, tail_pass, car_s)
    _sb_finish(o_ref, acc_s)


def _sb_scratch(tq):
    rows = D_HEADS * tq
    return [
        pltpu.VMEM((rows, LANES), BF16),
        pltpu.VMEM((rows, LANES), F32),
        pltpu.VMEM((rows, LANES), F32),
        pltpu.VMEM((2 * KEY_BLOCK, 2 * KEY_BLOCK), BF16),
        pltpu.VMEM((rows, 2 * KEY_BLOCK), F32),
        pltpu.VMEM((rows, 2 * KEY_BLOCK), BF16),
        pltpu.VMEM((rows, 2 * KEY_BLOCK), BF16),
        pltpu.VMEM((rows, 2 * KEY_BLOCK), BF16),
    ]


def _sb_self(q, kb, vb):
    nb, t, _ = q.shape
    assert t % KEY_BLOCK == 0
    full = pl.BlockSpec((1, t, D_BRANCH), lambda b, i: (b, 0, 0))
    blk = pl.BlockSpec((1, KEY_BLOCK, D_BRANCH), lambda b, i: (b, i, 0))
    return pl.pallas_call(
        _sb_self_kernel,
        grid=(nb, t // KEY_BLOCK),
        in_specs=[blk, full, full],
        out_specs=blk,
        out_shape=jax.ShapeDtypeStruct((nb, t, D_BRANCH), F32),
        scratch_shapes=_sb_scratch(KEY_BLOCK),
        compiler_params=pltpu.CompilerParams(
            dimension_semantics=("arbitrary", "arbitrary"),
            vmem_limit_bytes=VMEM_LIMIT_BYTES),
        name="sb_self",
    )(q, kb, vb)


def _sb_cached(q, kb_new, vb_new, k_past, v_past):
    nb, tq, _ = q.shape
    t_past = k_past.shape[1]
    assert tq <= KEY_BLOCK and tq % (2 * SUBLANES) == 0 and t_past % KEY_BLOCK == 0
    new = pl.BlockSpec((1, tq, D_BRANCH), lambda b: (b, 0, 0))
    past = pl.BlockSpec((1, t_past, D_BRANCH), lambda b: (b, 0, 0))
    return pl.pallas_call(
        functools.partial(_sb_cached_kernel, n_past_blocks=t_past // KEY_BLOCK),
        grid=(nb,),
        in_specs=[new, new, new, past, past],
        out_specs=new,
        out_shape=jax.ShapeDtypeStruct((nb, tq, D_BRANCH), F32),
        scratch_shapes=_sb_scratch(tq) + [
            pltpu.VMEM((2 * KEY_BLOCK, D_BRANCH), BF16),
            pltpu.VMEM((2 * KEY_BLOCK, D_BRANCH), BF16),
        ],
        compiler_params=pltpu.CompilerParams(
            dimension_semantics=("arbitrary",),
            vmem_limit_bytes=VMEM_LIMIT_BYTES),
        name="sb_cached",
    )(q, kb_new, vb_new, k_past, v_past)


def _cd_out_kernel(h_ref, cout_ref, o_ref, sdz_ref, p_ref, wout_ref, gate_ref, pproj_ref, fg_ref,
                   y_ref, *, bb, tt):
    m = bb * tt
    x = h_ref[...].reshape(m, D_MODEL)
    d_out = (o_ref[...] * sdz_ref[...]).reshape(m, D_BRANCH).astype(BF16)
    c_out = cout_ref[...].reshape(m, D_BRANCH)
    h = x + _dot(c_out, wout_ref[0:D_BRANCH, :]) + _dot(d_out, wout_ref[D_BRANCH:, :])
    gate = jax.nn.sigmoid(_dot(h.astype(BF16), gate_ref[...]))
    emb = _dot(p_ref[...].reshape(m, PLE_DIM).astype(BF16), pproj_ref[...])
    y_ref[...] = _rms(h + gate * emb, fg_ref[...]).reshape(bb, tt, D_MODEL)


def _cd_out(h, cout, o, sdz, p_all, layer, w_out, gate, pproj, fg, *, bb, tt):
    nb, t, _ = h.shape
    grid = (nb // bb, t // tt)
    tok = lambda d: pl.BlockSpec((bb, tt, d), lambda b, s: (b, s, 0))
    return pl.pallas_call(
        functools.partial(_cd_out_kernel, bb=bb, tt=tt),
        grid=grid,
        in_specs=[
            tok(D_MODEL), tok(D_BRANCH), tok(D_BRANCH), tok(D_BRANCH),
            pl.BlockSpec((None, bb, tt, PLE_DIM), lambda b, s: (layer, b, s, 0)),
            _const_spec((2 * D_BRANCH, D_MODEL)),
            _const_spec((D_MODEL, D_MODEL)),
            _const_spec((PLE_DIM, D_MODEL)),
            _const_spec((1, D_MODEL)),
        ],
        out_specs=tok(D_MODEL),
        out_shape=jax.ShapeDtypeStruct((nb, t, D_MODEL), F32),
        compiler_params=pltpu.CompilerParams(
            dimension_semantics=("arbitrary", "arbitrary"),
            vmem_limit_bytes=VMEM_LIMIT_BYTES),
        name="cd_out",
    )(h, cout, o, sdz, p_all, w_out, gate, pproj, fg)


def _trunk(x, p_all, hist_a, hist_b, k_past, v_past, w, *, bb, tt):
    t = x.shape[1]
    chunk = min(t, C_CHUNK)
    h1, new_a, new_b = _ab_layer(
        x, p_all, 0, hist_a, hist_b, w["norm_g"][0], w["ab_w_in"], w["a_conv_w"], w["b_conv_w"],
        w["b_ln_g"], w["b_ln_b"], w["ab_w_out"], w["ple_gate"][0], w["ple_proj"][0], bb=bb, tt=tt)
    cout, q, k, v, kb, vb, sdz, cvn = _cd_proj(
        h1, w["norm_g"][1], w["cd_w_in"], w["c_ln_g"], w["c_ln_b"],
        w["c_ws"][:, :chunk, :chunk], w["c_bt"][:chunk], bb=bb, tt=tt, chunk=chunk)
    if k_past is None:
        o = _sb_self(q, kb, vb)
    else:
        o = _sb_cached(q, kb, vb, k_past, v_past)
    y = _cd_out(h1, cout, o, sdz, p_all, 1, w["cd_w_out"], w["ple_gate"][1], w["ple_proj"][1],
                w["final_g"], bb=bb, tt=tt)
    return y, new_a, new_b, cvn, k, v


def kernel(x_prompt, x_sample, state_a_conv, state_b_conv, cache_d_k, cache_d_v, p_prompt, p_sample,
           norm_g, ple_gate, ple_proj, ab_w_in, a_conv_w, b_conv_w, b_ln_g, b_ln_b, ab_w_out,
           cd_w_in, c_ln_g, c_ln_b, c_ws, c_b, cd_w_out, final_g):
    bp, tp, _ = x_prompt.shape
    bs, ts, _ = x_sample.shape
    w = {
        "norm_g": norm_g.reshape(2, 1, D_MODEL),
        "ple_gate": ple_gate.astype(BF16),
        "ple_proj": ple_proj.astype(BF16),
        "ab_w_in": ab_w_in[0].astype(BF16),
        "a_conv_w": a_conv_w[0],
        "b_conv_w": b_conv_w[0],
        "b_ln_g": b_ln_g, "b_ln_b": b_ln_b,
        "ab_w_out": ab_w_out[0].astype(BF16),
        "cd_w_in": cd_w_in[0].astype(BF16),
        "c_ln_g": c_ln_g, "c_ln_b": c_ln_b,
        "c_ws": c_ws[0],
        "c_bt": c_b[0].T,
        "cd_w_out": cd_w_out[0].astype(BF16),
        "final_g": final_g.reshape(1, D_MODEL),
    }
    zeros_a = jnp.zeros((bp, HIST_A, D_BRANCH), F32)
    zeros_b = jnp.zeros((bp, HIST_B, D_BRANCH), F32)
    yp, pa, pb, _, pk, pv = _trunk(x_prompt, p_prompt, zeros_a, zeros_b, None, None, w, bb=1, tt=256)
    k_past = cache_d_k[0].reshape(bs, -1, D_BRANCH)
    v_past = cache_d_v[0].reshape(bs, -1, D_BRANCH)
    ys, sa, sb, scv, sk, sv = _trunk(x_sample, p_sample, state_a_conv[0], state_b_conv[0],
                                     k_past, v_past, w, bb=8, tt=ts)
    heads = lambda a: a.reshape(1, a.shape[0], a.shape[1], D_HEADS, D_HEAD_DIM)
    return (yp, ys, pa[None], sa[None], pb[None], sb[None], scv[None],
            heads(pk), heads(pv), heads(sk), heads(sv))
```

```python
import functools

import jax
import jax.numpy as jnp
from jax import lax
from jax.experimental import pallas as pl
from jax.experimental.pallas import tpu as pltpu

D_MODEL = 1024
PLE_DIM = 256
D_BRANCH = 512
CONV_A = 3
CONV_B = 31
C_HEADS = 4
C_CHUNK = 128
C_HEAD_DIM = D_BRANCH // C_HEADS
D_HEADS = 8
D_HEAD_DIM = D_BRANCH // D_HEADS
EPS = 1e-6

LANES = 128
SUBLANES = 8
VMEM_LIMIT_BYTES = 56 * 1024 * 1024

HIST_A = CONV_A - 1
HIST_B = CONV_B - 1
PAD_A = SUBLANES
PAD_B = 4 * SUBLANES
CONV_ROWS = 32
KEY_BLOCK = 128
STICK_LOG_FLOOR = -88.0

F32 = jnp.float32
BF16 = jnp.bfloat16


def _rms(x, g):
    return x * lax.rsqrt(jnp.mean(x * x, axis=-1, keepdims=True) + EPS) * g


def _layernorm(x, g, b):
    mu = jnp.mean(x, axis=-1, keepdims=True)
    xc = x - mu
    return xc * lax.rsqrt(jnp.mean(xc * xc, axis=-1, keepdims=True) + EPS) * g + b


def _silu(x):
    return x * jax.nn.sigmoid(x)


def _dot(a, b):
    return jnp.dot(a, b, preferred_element_type=F32)


def _ab_layer_kernel(x_ref, p_ref, ha_ref, hb_ref, g_ref, win_ref, aw_ref, bw_ref,
                     lng_ref, lnb_ref, wout_ref, gate_ref, pproj_ref,
                     h_ref, na_ref, nb_ref,
                     sh_s, *bufs, bb, tt, n_sub):
    s = pl.program_id(1)
    ts = tt // n_sub
    msub = bb * ts
    ua, ub, pr, ab, xb, hs = (bufs[i * n_sub:(i + 1) * n_sub] for i in range(6))
    hist_a = slice(PAD_A - HIST_A, PAD_A)
    hist_b = slice(PAD_B - HIST_B, PAD_B)
    tail_a = slice(PAD_A + ts - HIST_A, PAD_A + ts)
    tail_b = slice(PAD_B + ts - HIST_B, PAD_B + ts)

    @pl.when(s == 0)
    def _load_history():
        ua[0][:, hist_a, :] = ha_ref[...]
        ub[0][:, hist_b, :] = hb_ref[...]

    def col(i):
        return slice(i * D_BRANCH, (i + 1) * D_BRANCH)

    def project_jobs(k):
        t0 = k * ts

        def normalise():
            x = x_ref[:, t0:t0 + ts, :].reshape(msub, D_MODEL)
            xb[k][...] = _rms(x, g_ref[...]).astype(BF16)

        def proj(i):
            pr[k][:, col(i)] = _dot(xb[k][...], win_ref[:, col(i)])

        return [normalise] + [functools.partial(proj, i) for i in range(7)]

    def conv_jobs(k):
        def conv_inputs():
            ua[k][:, PAD_A:, :] = (pr[k][:, col(1)] * pr[k][:, col(0)]).reshape(bb, ts, D_BRANCH)
            ub[k][:, PAD_B:, :] = (
                pr[k][:, col(4)] * jax.nn.sigmoid(pr[k][:, col(5)])).reshape(bb, ts, D_BRANCH)
            if k > 0:
                ua[k][:, hist_a, :] = ua[k - 1][:, tail_a, :]
                ub[k][:, hist_b, :] = ub[k - 1][:, tail_b, :]

        def shifted_copies(b):
            for shift in range(1, SUBLANES):
                sh_s[shift - 1] = ub[k][b, shift:shift + PAD_B + ts - SUBLANES, :]

        def chunk(b, r0):
            if b == 0 and r0 == 0:
                conv_inputs()
            if r0 == 0:
                shifted_copies(b)
            rows = pl.ds(b * ts + r0, CONV_ROWS)
            conv_a = None
            for j in range(CONV_A):
                term = aw_ref[j:j + 1, :] * ua[k][b, pl.ds(PAD_A - HIST_A + r0 + j, CONV_ROWS), :]
                conv_a = term if conv_a is None else conv_a + term
            ab[k][rows, 0:D_BRANCH] = (pr[k][rows, col(2)] * _silu(pr[k][rows, col(3)]) * conv_a).astype(BF16)

            conv_b = None
            for j in range(CONV_B):
                blk, shift = divmod(PAD_B - HIST_B + j, SUBLANES)
                src = pl.ds(r0 + blk * SUBLANES, CONV_ROWS)
                tap = ub[k][b, src, :] if shift == 0 else sh_s[shift - 1, src, :]
                term = bw_ref[j:j + 1, :] * tap
                conv_b = term if conv_b is None else conv_b + term
            b_act = _silu(_layernorm(conv_b, lng_ref[...], lnb_ref[...]))
            ab[k][rows, D_BRANCH:2 * D_BRANCH] = (b_act * _silu(pr[k][rows, col(6)])).astype(BF16)

        return [functools.partial(chunk, b, r0) for b in range(bb) for r0 in range(0, ts, CONV_ROWS)]

    def finish_jobs(k):
        t0 = k * ts

        def residual():
            x = x_ref[:, t0:t0 + ts, :].reshape(msub, D_MODEL)
            hs[k][...] = x + _dot(ab[k][...], wout_ref[...])

        def embed():
            h = hs[k][...]
            gate = jax.nn.sigmoid(_dot(h.astype(BF16), gate_ref[...]))
            emb = _dot(p_ref[:, t0:t0 + ts, :].reshape(msub, PLE_DIM).astype(BF16), pproj_ref[...])
            h_ref[:, t0:t0 + ts, :] = (h + gate * emb).reshape(bb, ts, D_MODEL)

        return [residual, embed]

    def interleave(vpu_jobs, mxu_jobs):
        nv, nm = len(vpu_jobs), len(mxu_jobs)
        done = 0
        for i, job in enumerate(vpu_jobs):
            while done < nm and done * nv <= i * nm:
                mxu_jobs[done]()
                done += 1
            job()
        for job in mxu_jobs[done:]:
            job()

    interleave([], project_jobs(0))
    for k in range(n_sub):
        mxu_jobs = (finish_jobs(k - 1) if k > 0 else []) + (project_jobs(k + 1) if k + 1 < n_sub else [])
        interleave(conv_jobs(k), mxu_jobs)
    interleave([], finish_jobs(n_sub - 1))

    last_a = ua[n_sub - 1][:, tail_a, :]
    last_b = ub[n_sub - 1][:, tail_b, :]
    ua[0][:, hist_a, :] = last_a
    ub[0][:, hist_b, :] = last_b

    @pl.when(s == pl.num_programs(1) - 1)
    def _store_state():
        na_ref[...] = last_a
        nb_ref[...] = last_b


def _const_spec(shape):
    return pl.BlockSpec(shape, lambda b, s: (0,) * len(shape))


def _ab_layer(x, p_all, layer, hist_a, hist_b, g, w_in, a_w, b_w, ln_g, ln_b, w_out, gate, pproj,
              *, bb, tt, n_sub):
    nb, t, _ = x.shape
    assert nb % bb == 0 and t % tt == 0 and tt % n_sub == 0 and (bb == 1 or n_sub == 1)
    ts = tt // n_sub
    assert ts % CONV_ROWS == 0 and ts >= HIST_B
    msub = bb * ts
    grid = (nb // bb, t // tt)
    tok = lambda d: pl.BlockSpec((bb, tt, d), lambda b, s: (b, s, 0))
    state = lambda r: pl.BlockSpec((bb, r, D_BRANCH), lambda b, s: (b, 0, 0))
    return pl.pallas_call(
        functools.partial(_ab_layer_kernel, bb=bb, tt=tt, n_sub=n_sub),
        grid=grid,
        in_specs=[
            tok(D_MODEL),
            pl.BlockSpec((None, bb, tt, PLE_DIM), lambda b, s: (layer, b, s, 0)),
            state(HIST_A), state(HIST_B),
            _const_spec((1, D_MODEL)),
            _const_spec((D_MODEL, 7 * D_BRANCH)),
            _const_spec((CONV_A, D_BRANCH)),
            _const_spec((CONV_B, D_BRANCH)),
            _const_spec((1, D_BRANCH)), _const_spec((1, D_BRANCH)),
            _const_spec((2 * D_BRANCH, D_MODEL)),
            _const_spec((D_MODEL, D_MODEL)),
            _const_spec((PLE_DIM, D_MODEL)),
        ],
        out_specs=[tok(D_MODEL), state(HIST_A), state(HIST_B)],
        out_shape=[
            jax.ShapeDtypeStruct((nb, t, D_MODEL), F32),
            jax.ShapeDtypeStruct((nb, HIST_A, D_BRANCH), F32),
            jax.ShapeDtypeStruct((nb, HIST_B, D_BRANCH), F32),
        ],
        scratch_shapes=[pltpu.VMEM((SUBLANES - 1, PAD_B + ts - SUBLANES, D_BRANCH), F32)] + [
            pltpu.VMEM(shape, dtype)
            for shape, dtype in [
                ((bb, PAD_A + ts, D_BRANCH), F32),
                ((bb, PAD_B + ts, D_BRANCH), F32),
                ((msub, 7 * D_BRANCH), F32),
                ((msub, 2 * D_BRANCH), BF16),
                ((msub, D_MODEL), BF16),
                ((msub, D_MODEL), F32),
            ]
            for _ in range(n_sub)
        ],
        compiler_params=pltpu.CompilerParams(
            dimension_semantics=("arbitrary", "arbitrary"),
            vmem_limit_bytes=VMEM_LIMIT_BYTES),
        name="ab_layer",
    )(x, p_all, hist_a, hist_b, g, w_in, a_w, b_w, ln_g, ln_b, w_out, gate, pproj)


def _cd_proj_kernel(h_ref, g_ref, win_ref, lng_ref, lnb_ref, ws_ref, cbt_ref,
                    cout_ref, q_ref, k_ref, v_ref, kb_ref, vb_ref, sdz_ref, *rest, bb, tt, chunk):
    cvn_s, gc_s = rest[-2:]
    m = bb * tt
    x = h_ref[...].reshape(m, D_MODEL)
    xb = _rms(x, g_ref[...]).astype(BF16)

    def proj(i):
        return _dot(xb, win_ref[:, i * D_BRANCH:(i + 1) * D_BRANCH])

    gc_s[...] = proj(0) * _silu(proj(2))
    cvn = _layernorm(proj(1), lng_ref[...], lnb_ref[...])
    if len(rest) == 3:
        rest[0][...] = cvn.reshape(bb, tt, D_BRANCH)
    cvn_s[...] = cvn.astype(BF16)
    q_ref[...] = (proj(3) * (D_HEAD_DIM ** -0.5)).astype(BF16).reshape(bb, tt, D_BRANCH)
    k = proj(4).reshape(bb, tt, D_BRANCH)
    k_ref[...] = k
    kb_ref[...] = k.astype(BF16)
    v = proj(5).reshape(bb, tt, D_BRANCH)
    v_ref[...] = v
    vb_ref[...] = v.astype(BF16)
    sdz_ref[...] = _silu(proj(6)).reshape(bb, tt, D_BRANCH)

    row = lax.broadcasted_iota(jnp.int32, (chunk, chunk), 0)
    col = lax.broadcasted_iota(jnp.int32, (chunk, chunk), 1)
    for hd in range(C_HEADS):
        lanes = slice(hd * C_HEAD_DIM, (hd + 1) * C_HEAD_DIM)
        w_tril = jnp.where(col <= row, ws_ref[hd], 0.0).astype(BF16)
        bias = cbt_ref[:, hd:hd + 1]
        for c0 in range(0, m, chunk):
            rows = pl.ds(c0, chunk)
            mixed = _dot(w_tril, cvn_s[rows, lanes]) + bias
            cout_ref[c0 // tt, pl.ds(c0 % tt, chunk), lanes] = (gc_s[rows, lanes] * mixed).astype(BF16)


def _cd_proj(h, g, w_in, ln_g, ln_b, ws, cbt, *, bb, tt, chunk, want_cvn):
    nb, t, _ = h.shape
    assert nb % bb == 0 and t % tt == 0 and tt % chunk == 0
    m = bb * tt
    grid = (nb // bb, t // tt)
    tok = lambda d: pl.BlockSpec((bb, tt, d), lambda b, s: (b, s, 0))
    shp = lambda dt: jax.ShapeDtypeStruct((nb, t, D_BRANCH), dt)
    return pl.pallas_call(
        functools.partial(_cd_proj_kernel, bb=bb, tt=tt, chunk=chunk),
        grid=grid,
        in_specs=[
            tok(D_MODEL),
            _const_spec((1, D_MODEL)),
            _const_spec((D_MODEL, 7 * D_BRANCH)),
            _const_spec((1, D_BRANCH)), _const_spec((1, D_BRANCH)),
            _const_spec((C_HEADS, chunk, chunk)),
            _const_spec((chunk, C_HEADS)),
        ],
        out_specs=[tok(D_BRANCH)] * (8 if want_cvn else 7),
        out_shape=[shp(BF16), shp(BF16), shp(F32), shp(F32), shp(BF16), shp(BF16), shp(F32)]
        + ([shp(F32)] if want_cvn else []),
        scratch_shapes=[
            pltpu.VMEM((m, D_BRANCH), BF16),
            pltpu.VMEM((m, D_BRANCH), F32),
        ],
        compiler_params=pltpu.CompilerParams(
            dimension_semantics=("arbitrary", "arbitrary"),
            vmem_limit_bytes=VMEM_LIMIT_BYTES),
        name="cd_proj",
    )(h, g, w_in, ln_g, ln_b, ws, cbt)


def _dot_nt(a, b):
    return lax.dot_general(a, b, (((1,), (1,)), ((), ())), preferred_element_type=F32)


def _pair_lanes(pr):
    return slice(pr * LANES, (pr + 1) * LANES)


def _sb_setup(q, qm_s, tri_s):
    tq = q.shape[0]
    lane = lax.broadcasted_iota(jnp.int32, (tq, LANES), 1)
    for hd in range(D_HEADS):
        pair = q[:, _pair_lanes(hd // 2)]
        in_head = (lane >= D_HEAD_DIM) if hd % 2 else (lane < D_HEAD_DIM)
        qm_s[hd * tq:(hd + 1) * tq, :] = jnp.where(in_head, pair, jnp.zeros_like(pair))
    n = tri_s.shape[0]
    row = lax.broadcasted_iota(jnp.int32, (n, n), 0)
    col = lax.broadcasted_iota(jnp.int32, (n, n), 1)
    tri_s[...] = jnp.where(row > col, 1.0, 0.0).astype(BF16)


def _sb_first_mask(tq, prev_ok):
    shape = (2 * tq, 2 * KEY_BLOCK)
    r = lax.broadcasted_iota(jnp.int32, shape, 0)
    qi = jnp.where(r >= tq, r - tq, r)
    c = lax.broadcasted_iota(jnp.int32, shape, 1)
    causal = (c - KEY_BLOCK) < qi
    if prev_ok is None:
        return causal
    return causal & ((c >= KEY_BLOCK) | (jnp.broadcast_to(prev_ok, shape) > 0))


def _sb_pass(qk, av, nk, valid, first, qm_s, acc_s, car_s, tri_s, lb_s, hi_s, lo_s, a_s):
    m2 = qm_s.shape[0] // (D_HEADS // 2)
    pairs = [pl.ds(pr * m2, m2) for pr in range(D_HEADS // 2)]
    tri = tri_s[0:nk, 0:nk]
    for pr, rows in enumerate(pairs):
        z = qk(pr, qm_s[rows, :])
        soft = jnp.log(1.0 + jnp.exp(-jnp.abs(z)))
        log_beta = jnp.minimum(z, 0.0) - soft
        log_keep = log_beta - z
        if valid is not None:
            log_keep = jnp.where(valid, log_keep, 0.0)
        hi = log_keep.astype(BF16)
        hi_s[rows, 0:nk] = hi
        lo_s[rows, 0:nk] = (log_keep - hi.astype(F32)).astype(BF16)
        total = jnp.sum(log_keep, axis=-1, keepdims=True)
        if first:
            lb_s[rows, 0:nk] = log_beta
            car_s[rows, :] = jnp.broadcast_to(total, (m2, LANES))
        else:
            carry = car_s[rows, :]
            lb_s[rows, 0:nk] = log_beta + jnp.concatenate([carry] * (nk // LANES), axis=1)
            car_s[rows, :] = carry + total
    for pr, rows in enumerate(pairs):
        later = _dot(hi_s[rows, 0:nk], tri) + _dot(lo_s[rows, 0:nk], tri)
        a = jnp.exp(lb_s[rows, 0:nk] + later)
        if valid is not None:
            a = jnp.where(valid, a, 0.0)
        a_s[rows, 0:nk] = a.astype(BF16)
    for pr, rows in enumerate(pairs):
        pv = av(pr, a_s[rows, 0:nk])
        acc_s[rows, :] = pv if first else acc_s[rows, :] + pv


def _sb_all_spent(car_s):
    return (jnp.max(car_s[...]) < STICK_LOG_FLOOR).astype(jnp.int32)


def _sb_sweep(first_block, tail_pass, car_s):
    def cond(c):
        j, spent = c
        return jnp.logical_and(j >= 0, spent == 0)

    def body(c):
        j, _ = c
        tail_pass(j)
        return j - 1, _sb_all_spent(car_s)

    lax.while_loop(cond, body, (jnp.asarray(first_block, jnp.int32), _sb_all_spent(car_s)))


def _sb_finish(o_ref, acc_s):
    tq = acc_s.shape[0] // D_HEADS
    lane = lax.broadcasted_iota(jnp.int32, (tq, LANES), 1)
    for pr in range(D_HEADS // 2):
        even = acc_s[(2 * pr) * tq:(2 * pr + 1) * tq, :]
        odd = acc_s[(2 * pr + 1) * tq:(2 * pr + 2) * tq, :]
        o_ref[0, :, _pair_lanes(pr)] = jnp.where(lane < D_HEAD_DIM, even, odd)


def _key_rows(j):
    return pl.ds(pl.multiple_of(j * KEY_BLOCK, KEY_BLOCK), KEY_BLOCK)


def _sb_self_kernel(q_ref, k_ref, v_ref, o_ref, qm_s, acc_s, car_s, tri_s, lb_s, hi_s, lo_s, a_s):
    i = pl.program_id(1)
    tq = q_ref.shape[1]
    scratch = (qm_s, acc_s, car_s, tri_s, lb_s, hi_s, lo_s, a_s)
    _sb_setup(q_ref[0], qm_s, tri_s)
    prev, diag = _key_rows(jnp.maximum(i - 1, 0)), _key_rows(i)

    def both(ref, pr):
        return jnp.concatenate([ref[0, prev, _pair_lanes(pr)], ref[0, diag, _pair_lanes(pr)]], axis=0)

    _sb_pass(lambda pr, q: _dot_nt(q, both(k_ref, pr)), lambda pr, a: _dot(a, both(v_ref, pr)),
             2 * KEY_BLOCK, _sb_first_mask(tq, i), True, *scratch)

    def tail_pass(j):
        rows = _key_rows(j)
        _sb_pass(lambda pr, q: _dot_nt(q, k_ref[0, rows, _pair_lanes(pr)]),
                 lambda pr, a: _dot(a, v_ref[0, rows, _pair_lanes(pr)]),
                 KEY_BLOCK, None, False, *scratch)

    _sb_sweep(i - 2, tail_pass, car_s)
    _sb_finish(o_ref, acc_s)


def _sb_cached_kernel(q_ref, kn_ref, vn_ref, kl_ref, vl_ref, kc_hbm, vc_hbm, o_ref,
                      qm_s, acc_s, car_s, tri_s, lb_s, hi_s, lo_s, a_s, kd_s, vd_s, kt_s, vt_s,
                      *, n_past_blocks):
    b = pl.program_id(0)
    tq = q_ref.shape[1]
    tn = kn_ref.shape[1]
    scratch = (qm_s, acc_s, car_s, tri_s, lb_s, hi_s, lo_s, a_s)
    _sb_setup(q_ref[0], qm_s, tri_s)
    kd_s[...] = jnp.zeros_like(kd_s)
    vd_s[...] = jnp.zeros_like(vd_s)
    kd_s[0:tn, :] = kn_ref[0]
    vd_s[0:tn, :] = vn_ref[0]

    def pair_t(ref, pr):
        return ref[2 * pr:2 * pr + 2].reshape(LANES, KEY_BLOCK).astype(BF16)

    def qk_first(pr, q):
        return jnp.concatenate([_dot(q, pair_t(kl_ref.at[0], pr)), _dot_nt(q, kd_s[:, _pair_lanes(pr)])], axis=1)

    def av_first(pr, a):
        return _dot_nt(a[:, :KEY_BLOCK], pair_t(vl_ref.at[0], pr)) + _dot(a[:, KEY_BLOCK:], vd_s[:, _pair_lanes(pr)])

    _sb_pass(qk_first, av_first, 2 * KEY_BLOCK, _sb_first_mask(tq, None), True, *scratch)

    def tail_pass(j):
        cols = _key_rows(j)
        pltpu.sync_copy(kc_hbm.at[b, :, :, cols], kt_s)
        pltpu.sync_copy(vc_hbm.at[b, :, :, cols], vt_s)
        _sb_pass(lambda pr, q: _dot(q, pair_t(kt_s, pr)), lambda pr, a: _dot_nt(a, pair_t(vt_s, pr)),
                 KEY_BLOCK, None, False, *scratch)

    _sb_sweep(n_past_blocks - 2, tail_pass, car_s)
    _sb_finish(o_ref, acc_s)


def _sb_scratch(tq):
    rows = D_HEADS * tq
    return [
        pltpu.VMEM((rows, LANES), BF16),
        pltpu.VMEM((rows, LANES), F32),
        pltpu.VMEM((rows, LANES), F32),
        pltpu.VMEM((2 * KEY_BLOCK, 2 * KEY_BLOCK), BF16),
        pltpu.VMEM((rows, 2 * KEY_BLOCK), F32),
        pltpu.VMEM((rows, 2 * KEY_BLOCK), BF16),
        pltpu.VMEM((rows, 2 * KEY_BLOCK), BF16),
        pltpu.VMEM((rows, 2 * KEY_BLOCK), BF16),
    ]


def _sb_self(q, kb, vb):
    nb, t, _ = q.shape
    assert t % KEY_BLOCK == 0
    full = pl.BlockSpec((1, t, D_BRANCH), lambda b, i: (b, 0, 0))
    blk = pl.BlockSpec((1, KEY_BLOCK, D_BRANCH), lambda b, i: (b, i, 0))
    return pl.pallas_call(
        _sb_self_kernel,
        grid=(nb, t // KEY_BLOCK),
        in_specs=[blk, full, full],
        out_specs=blk,
        out_shape=jax.ShapeDtypeStruct((nb, t, D_BRANCH), F32),
        scratch_shapes=_sb_scratch(KEY_BLOCK),
        compiler_params=pltpu.CompilerParams(
            dimension_semantics=("arbitrary", "arbitrary"),
            vmem_limit_bytes=VMEM_LIMIT_BYTES),
        name="sb_self",
    )(q, kb, vb)


def _sb_cached(q, kb_new, vb_new, k_cache, v_cache):
    nb, tq, _ = q.shape
    t_past = k_cache.shape[-1]
    assert tq <= KEY_BLOCK and tq % (2 * SUBLANES) == 0 and t_past % KEY_BLOCK == 0
    n_past_blocks = t_past // KEY_BLOCK
    new = pl.BlockSpec((1, tq, D_BRANCH), lambda b: (b, 0, 0))
    last = pl.BlockSpec((1, D_HEADS, D_HEAD_DIM, KEY_BLOCK), lambda b: (b, 0, 0, n_past_blocks - 1))
    anywhere = pl.BlockSpec(memory_space=pl.ANY)
    cache_blk = pltpu.VMEM((D_HEADS, D_HEAD_DIM, KEY_BLOCK), F32)
    return pl.pallas_call(
        functools.partial(_sb_cached_kernel, n_past_blocks=n_past_blocks),
        grid=(nb,),
        in_specs=[new, new, new, last, last, anywhere, anywhere],
        out_specs=new,
        out_shape=jax.ShapeDtypeStruct((nb, tq, D_BRANCH), F32),
        scratch_shapes=_sb_scratch(tq) + [
            pltpu.VMEM((KEY_BLOCK, D_BRANCH), BF16),
            pltpu.VMEM((KEY_BLOCK, D_BRANCH), BF16),
            cache_blk, cache_blk,
        ],
        compiler_params=pltpu.CompilerParams(
            dimension_semantics=("arbitrary",),
            vmem_limit_bytes=VMEM_LIMIT_BYTES),
        name="sb_cached",
    )(q, kb_new, vb_new, k_cache, v_cache, k_cache, v_cache)


def _cd_out_kernel(h_ref, cout_ref, o_ref, sdz_ref, p_ref, wout_ref, gate_ref, pproj_ref, fg_ref,
                   y_ref, *, bb, tt):
    m = bb * tt
    x = h_ref[...].reshape(m, D_MODEL)
    d_out = (o_ref[...] * sdz_ref[...]).reshape(m, D_BRANCH).astype(BF16)
    c_out = cout_ref[...].reshape(m, D_BRANCH)
    h = x + _dot(c_out, wout_ref[0:D_BRANCH, :]) + _dot(d_out, wout_ref[D_BRANCH:, :])
    gate = jax.nn.sigmoid(_dot(h.astype(BF16), gate_ref[...]))
    emb = _dot(p_ref[...].reshape(m, PLE_DIM).astype(BF16), pproj_ref[...])
    y_ref[...] = _rms(h + gate * emb, fg_ref[...]).reshape(bb, tt, D_MODEL)


def _cd_out(h, cout, o, sdz, p_all, layer, w_out, gate, pproj, fg, *, bb, tt):
    nb, t, _ = h.shape
    grid = (nb // bb, t // tt)
    tok = lambda d: pl.BlockSpec((bb, tt, d), lambda b, s: (b, s, 0))
    return pl.pallas_call(
        functools.partial(_cd_out_kernel, bb=bb, tt=tt),
        grid=grid,
        in_specs=[
            tok(D_MODEL), tok(D_BRANCH), tok(D_BRANCH), tok(D_BRANCH),
            pl.BlockSpec((None, bb, tt, PLE_DIM), lambda b, s: (layer, b, s, 0)),
            _const_spec((2 * D_BRANCH, D_MODEL)),
            _const_spec((D_MODEL, D_MODEL)),
            _const_spec((PLE_DIM, D_MODEL)),
            _const_spec((1, D_MODEL)),
        ],
        out_specs=tok(D_MODEL),
        out_shape=jax.ShapeDtypeStruct((nb, t, D_MODEL), F32),
        compiler_params=pltpu.CompilerParams(
            dimension_semantics=("arbitrary", "arbitrary"),
            vmem_limit_bytes=VMEM_LIMIT_BYTES),
        name="cd_out",
    )(h, cout, o, sdz, p_all, w_out, gate, pproj, fg)


def _trunk(x, p_all, hist_a, hist_b, k_cache, v_cache, w, *, bb, tt, n_sub):
    t = x.shape[1]
    chunk = min(t, C_CHUNK)
    h1, new_a, new_b = _ab_layer(
        x, p_all, 0, hist_a, hist_b, w["norm_g"][0], w["ab_w_in"], w["a_conv_w"], w["b_conv_w"],
        w["b_ln_g"], w["b_ln_b"], w["ab_w_out"], w["ple_gate"][0], w["ple_proj"][0],
        bb=bb, tt=tt, n_sub=n_sub)
    tt1 = tt // n_sub
    cout, q, k, v, kb, vb, sdz, *cvn = _cd_proj(
        h1, w["norm_g"][1], w["cd_w_in"], w["c_ln_g"], w["c_ln_b"],
        w["c_ws"][:, :chunk, :chunk], w["c_bt"][:chunk], bb=bb, tt=tt1, chunk=chunk,
        want_cvn=k_cache is not None)
    if k_cache is None:
        o = _sb_self(q, kb, vb)
    else:
        o = _sb_cached(q, kb, vb, k_cache, v_cache)
    y = _cd_out(h1, cout, o, sdz, p_all, 1, w["cd_w_out"], w["ple_gate"][1], w["ple_proj"][1],
                w["final_g"], bb=bb, tt=tt1)
    return y, new_a, new_b, (cvn[0] if cvn else None), k, v


def kernel(x_prompt, x_sample, state_a_conv, state_b_conv, cache_d_k, cache_d_v, p_prompt, p_sample,
           norm_g, ple_gate, ple_proj, ab_w_in, a_conv_w, b_conv_w, b_ln_g, b_ln_b, ab_w_out,
           cd_w_in, c_ln_g, c_ln_b, c_ws, c_b, cd_w_out, final_g):
    bp, tp, _ = x_prompt.shape
    bs, ts, _ = x_sample.shape
    w = {
        "norm_g": norm_g.reshape(2, 1, D_MODEL),
        "ple_gate": ple_gate.astype(BF16),
        "ple_proj": ple_proj.astype(BF16),
        "ab_w_in": ab_w_in[0].astype(BF16),
        "a_conv_w": a_conv_w[0],
        "b_conv_w": b_conv_w[0],
        "b_ln_g": b_ln_g, "b_ln_b": b_ln_b,
        "ab_w_out": ab_w_out[0].astype(BF16),
        "cd_w_in": cd_w_in[0].astype(BF16),
        "c_ln_g": c_ln_g, "c_ln_b": c_ln_b,
        "c_ws": c_ws[0],
        "c_bt": c_b[0].T,
        "cd_w_out": cd_w_out[0].astype(BF16),
        "final_g": final_g.reshape(1, D_MODEL),
    }
    zeros_a = jnp.zeros((bp, HIST_A, D_BRANCH), F32)
    zeros_b = jnp.zeros((bp, HIST_B, D_BRANCH), F32)
    yp, pa, pb, _, pk, pv = _trunk(x_prompt, p_prompt, zeros_a, zeros_b, None, None, w,
                                   bb=1, tt=512, n_sub=2)
    k_cache = jnp.transpose(cache_d_k[0], (0, 2, 3, 1))
    v_cache = jnp.transpose(cache_d_v[0], (0, 2, 3, 1))
    ys, sa, sb, scv, sk, sv = _trunk(x_sample, p_sample, state_a_conv[0], state_b_conv[0],
                                     k_cache, v_cache, w, bb=8, tt=ts, n_sub=1)
    heads = lambda a: a.reshape(1, a.shape[0], a.shape[1], D_HEADS, D_HEAD_DIM)
    return (yp, ys, pa[None], sa[None], pb[None], sb[None], scv[None],
            heads(pk), heads(pv), heads(sk), heads(sv))
```

```python
import functools

import jax
import jax.numpy as jnp
from jax import lax
from jax.experimental import pallas as pl
from jax.experimental.pallas import tpu as pltpu

D_MODEL = 1024
PLE_DIM = 256
D_BRANCH = 512
CONV_A = 3
CONV_B = 31
C_HEADS = 4
C_CHUNK = 128
C_HEAD_DIM = D_BRANCH // C_HEADS
D_HEADS = 8
D_HEAD_DIM = D_BRANCH // D_HEADS
EPS = 1e-6

LANES = 128
SUBLANES = 8
VMEM_LIMIT_BYTES = 56 * 1024 * 1024

HIST_A = CONV_A - 1
HIST_B = CONV_B - 1
PAD_A = SUBLANES
PAD_B = 4 * SUBLANES
CONV_ROWS = 32
KEY_BLOCK = 128
STICK_LOG_FLOOR = -88.0

F32 = jnp.float32
BF16 = jnp.bfloat16


def _rms(x, g):
    return x * lax.rsqrt(jnp.mean(x * x, axis=-1, keepdims=True) + EPS) * g


def _layernorm(x, g, b):
    mu = jnp.mean(x, axis=-1, keepdims=True)
    xc = x - mu
    return xc * lax.rsqrt(jnp.mean(xc * xc, axis=-1, keepdims=True) + EPS) * g + b


def _silu(x):
    return x * jax.nn.sigmoid(x)


def _dot(a, b):
    return jnp.dot(a, b, preferred_element_type=F32)


def _ab_layer_kernel(x_ref, p_ref, ha_ref, hb_ref, g_ref, win_ref, aw_ref, bw_ref,
                     lng_ref, lnb_ref, wout_ref, gate_ref, pproj_ref,
                     h_ref, na_ref, nb_ref,
                     sh_s, wa_s, wb_s, *bufs, bb, tt, n_sub):
    s = pl.program_id(1)
    ts = tt // n_sub
    msub = bb * ts
    ua, ub, pr, ab, xb, hs = (bufs[i * n_sub:(i + 1) * n_sub] for i in range(6))
    hist_a = slice(PAD_A - HIST_A, PAD_A)
    hist_b = slice(PAD_B - HIST_B, PAD_B)
    tail_a = slice(PAD_A + ts - HIST_A, PAD_A + ts)
    tail_b = slice(PAD_B + ts - HIST_B, PAD_B + ts)

    @pl.when(s == 0)
    def _load_history():
        ua[0][:, hist_a, :] = ha_ref[...]
        ub[0][:, hist_b, :] = hb_ref[...]

    for j in range(CONV_A):
        wa_s[j] = jnp.broadcast_to(aw_ref[j:j + 1, :], (SUBLANES, D_BRANCH))
    for j in range(CONV_B):
        wb_s[j] = jnp.broadcast_to(bw_ref[j:j + 1, :], (SUBLANES, D_BRANCH))

    def rows_of(tile):
        return jnp.concatenate([tile] * (CONV_ROWS // SUBLANES), axis=0)

    def col(i):
        return slice(i * D_BRANCH, (i + 1) * D_BRANCH)

    def project_jobs(k):
        t0 = k * ts

        def normalise():
            x = x_ref[:, t0:t0 + ts, :].reshape(msub, D_MODEL)
            xb[k][...] = _rms(x, g_ref[...]).astype(BF16)

        def proj(i):
            pr[k][:, col(i)] = _dot(xb[k][...], win_ref[:, col(i)])

        return [normalise] + [functools.partial(proj, i) for i in range(7)]

    def conv_jobs(k):
        def conv_inputs():
            ua[k][:, PAD_A:, :] = (pr[k][:, col(1)] * pr[k][:, col(0)]).reshape(bb, ts, D_BRANCH)
            ub[k][:, PAD_B:, :] = (
                pr[k][:, col(4)] * jax.nn.sigmoid(pr[k][:, col(5)])).reshape(bb, ts, D_BRANCH)
            if k > 0:
                ua[k][:, hist_a, :] = ua[k - 1][:, tail_a, :]
                ub[k][:, hist_b, :] = ub[k - 1][:, tail_b, :]

        def shifted_copies(b):
            for shift in range(1, SUBLANES):
                sh_s[shift - 1] = ub[k][b, shift:shift + PAD_B + ts - SUBLANES, :]

        def chunk(b, r0):
            if b == 0 and r0 == 0:
                conv_inputs()
            if r0 == 0:
                shifted_copies(b)
            rows = pl.ds(b * ts + r0, CONV_ROWS)
            conv_a = None
            for j in range(CONV_A):
                term = rows_of(wa_s[j]) * ua[k][b, pl.ds(PAD_A - HIST_A + r0 + j, CONV_ROWS), :]
                conv_a = term if conv_a is None else conv_a + term
            ab[k][rows, 0:D_BRANCH] = (pr[k][rows, col(2)] * _silu(pr[k][rows, col(3)]) * conv_a).astype(BF16)

            conv_b = None
            for j in range(CONV_B):
                blk, shift = divmod(PAD_B - HIST_B + j, SUBLANES)
                src = pl.ds(r0 + blk * SUBLANES, CONV_ROWS)
                tap = ub[k][b, src, :] if shift == 0 else sh_s[shift - 1, src, :]
                term = rows_of(wb_s[j]) * tap
                conv_b = term if conv_b is None else conv_b + term
            b_act = _silu(_layernorm(conv_b, lng_ref[...], lnb_ref[...]))
            ab[k][rows, D_BRANCH:2 * D_BRANCH] = (b_act * _silu(pr[k][rows, col(6)])).astype(BF16)

        return [functools.partial(chunk, b, r0) for b in range(bb) for r0 in range(0, ts, CONV_ROWS)]

    def finish_jobs(k):
        t0 = k * ts

        def residual():
            x = x_ref[:, t0:t0 + ts, :].reshape(msub, D_MODEL)
            hs[k][...] = x + _dot(ab[k][...], wout_ref[...])

        def embed():
            h = hs[k][...]
            gate = jax.nn.sigmoid(_dot(h.astype(BF16), gate_ref[...]))
            emb = _dot(p_ref[:, t0:t0 + ts, :].reshape(msub, PLE_DIM).astype(BF16), pproj_ref[...])
            h_ref[:, t0:t0 + ts, :] = (h + gate * emb).reshape(bb, ts, D_MODEL)

        return [residual, embed]

    def interleave(vpu_jobs, mxu_jobs):
        nv, nm = len(vpu_jobs), len(mxu_jobs)
        done = 0
        for i, job in enumerate(vpu_jobs):
            while done < nm and done * nv <= i * nm:
                mxu_jobs[done]()
                done += 1
            job()
        for job in mxu_jobs[done:]:
            job()

    interleave([], project_jobs(0))
    for k in range(n_sub):
        mxu_jobs = (finish_jobs(k - 1) if k > 0 else []) + (project_jobs(k + 1) if k + 1 < n_sub else [])
        interleave(conv_jobs(k), mxu_jobs)
    interleave([], finish_jobs(n_sub - 1))

    last_a = ua[n_sub - 1][:, tail_a, :]
    last_b = ub[n_sub - 1][:, tail_b, :]
    ua[0][:, hist_a, :] = last_a
    ub[0][:, hist_b, :] = last_b

    @pl.when(s == pl.num_programs(1) - 1)
    def _store_state():
        na_ref[...] = last_a
        nb_ref[...] = last_b


def _const_spec(shape):
    return pl.BlockSpec(shape, lambda b, s: (0,) * len(shape))


def _ab_layer(x, p_all, layer, hist_a, hist_b, g, w_in, a_w, b_w, ln_g, ln_b, w_out, gate, pproj,
              *, bb, tt, n_sub):
    nb, t, _ = x.shape
    assert nb % bb == 0 and t % tt == 0 and tt % n_sub == 0 and (bb == 1 or n_sub == 1)
    ts = tt // n_sub
    assert ts % CONV_ROWS == 0 and ts >= HIST_B
    msub = bb * ts
    grid = (nb // bb, t // tt)
    tok = lambda d: pl.BlockSpec((bb, tt, d), lambda b, s: (b, s, 0))
    state = lambda r: pl.BlockSpec((bb, r, D_BRANCH), lambda b, s: (b, 0, 0))
    return pl.pallas_call(
        functools.partial(_ab_layer_kernel, bb=bb, tt=tt, n_sub=n_sub),
        grid=grid,
        in_specs=[
            tok(D_MODEL),
            pl.BlockSpec((None, bb, tt, PLE_DIM), lambda b, s: (layer, b, s, 0)),
            state(HIST_A), state(HIST_B),
            _const_spec((1, D_MODEL)),
            _const_spec((D_MODEL, 7 * D_BRANCH)),
            _const_spec((CONV_A, D_BRANCH)),
            _const_spec((CONV_B, D_BRANCH)),
            _const_spec((1, D_BRANCH)), _const_spec((1, D_BRANCH)),
            _const_spec((2 * D_BRANCH, D_MODEL)),
            _const_spec((D_MODEL, D_MODEL)),
            _const_spec((PLE_DIM, D_MODEL)),
        ],
        out_specs=[tok(D_MODEL), state(HIST_A), state(HIST_B)],
        out_shape=[
            jax.ShapeDtypeStruct((nb, t, D_MODEL), F32),
            jax.ShapeDtypeStruct((nb, HIST_A, D_BRANCH), F32),
            jax.ShapeDtypeStruct((nb, HIST_B, D_BRANCH), F32),
        ],
        scratch_shapes=[
            pltpu.VMEM((SUBLANES - 1, PAD_B + ts - SUBLANES, D_BRANCH), F32),
            pltpu.VMEM((CONV_A, SUBLANES, D_BRANCH), F32),
            pltpu.VMEM((CONV_B, SUBLANES, D_BRANCH), F32),
        ] + [
            pltpu.VMEM(shape, dtype)
            for shape, dtype in [
                ((bb, PAD_A + ts, D_BRANCH), F32),
                ((bb, PAD_B + ts, D_BRANCH), F32),
                ((msub, 7 * D_BRANCH), F32),
                ((msub, 2 * D_BRANCH), BF16),
                ((msub, D_MODEL), BF16),
                ((msub, D_MODEL), F32),
            ]
            for _ in range(n_sub)
        ],
        compiler_params=pltpu.CompilerParams(
            dimension_semantics=("arbitrary", "arbitrary"),
            vmem_limit_bytes=VMEM_LIMIT_BYTES),
        name="ab_layer",
    )(x, p_all, hist_a, hist_b, g, w_in, a_w, b_w, ln_g, ln_b, w_out, gate, pproj)


def _cd_proj_kernel(h_ref, g_ref, win_ref, lng_ref, lnb_ref, ws_ref, cbt_ref,
                    cout_ref, q_ref, k_ref, v_ref, kb_ref, vb_ref, sdz_ref, *rest, bb, tt, chunk):
    cvn_s, gc_s = rest[-2:]
    m = bb * tt
    x = h_ref[...].reshape(m, D_MODEL)
    xb = _rms(x, g_ref[...]).astype(BF16)

    def proj(i):
        return _dot(xb, win_ref[:, i * D_BRANCH:(i + 1) * D_BRANCH])

    gc_s[...] = proj(0) * _silu(proj(2))
    cvn = _layernorm(proj(1), lng_ref[...], lnb_ref[...])
    if len(rest) == 3:
        rest[0][...] = cvn.reshape(bb, tt, D_BRANCH)
    cvn_s[...] = cvn.astype(BF16)
    q_ref[...] = (proj(3) * (D_HEAD_DIM ** -0.5)).astype(BF16).reshape(bb, tt, D_BRANCH)
    k = proj(4).reshape(bb, tt, D_BRANCH)
    k_ref[...] = k
    kb_ref[...] = k.astype(BF16)
    v = proj(5).reshape(bb, tt, D_BRANCH)
    v_ref[...] = v
    vb_ref[...] = v.astype(BF16)
    sdz_ref[...] = _silu(proj(6)).reshape(bb, tt, D_BRANCH)

    row = lax.broadcasted_iota(jnp.int32, (chunk, chunk), 0)
    col = lax.broadcasted_iota(jnp.int32, (chunk, chunk), 1)
    for hd in range(C_HEADS):
        lanes = slice(hd * C_HEAD_DIM, (hd + 1) * C_HEAD_DIM)
        w_tril = jnp.where(col <= row, ws_ref[hd], 0.0).astype(BF16)
        bias = cbt_ref[:, hd:hd + 1]
        for c0 in range(0, m, chunk):
            rows = pl.ds(c0, chunk)
            mixed = _dot(w_tril, cvn_s[rows, lanes]) + bias
            cout_ref[c0 // tt, pl.ds(c0 % tt, chunk), lanes] = (gc_s[rows, lanes] * mixed).astype(BF16)


def _cd_proj(h, g, w_in, ln_g, ln_b, ws, cbt, *, bb, tt, chunk, want_cvn):
    nb, t, _ = h.shape
    assert nb % bb == 0 and t % tt == 0 and tt % chunk == 0
    m = bb * tt
    grid = (nb // bb, t // tt)
    tok = lambda d: pl.BlockSpec((bb, tt, d), lambda b, s: (b, s, 0))
    shp = lambda dt: jax.ShapeDtypeStruct((nb, t, D_BRANCH), dt)
    return pl.pallas_call(
        functools.partial(_cd_proj_kernel, bb=bb, tt=tt, chunk=chunk),
        grid=grid,
        in_specs=[
            tok(D_MODEL),
            _const_spec((1, D_MODEL)),
            _const_spec((D_MODEL, 7 * D_BRANCH)),
            _const_spec((1, D_BRANCH)), _const_spec((1, D_BRANCH)),
            _const_spec((C_HEADS, chunk, chunk)),
            _const_spec((chunk, C_HEADS)),
        ],
        out_specs=[tok(D_BRANCH)] * (8 if want_cvn else 7),
        out_shape=[shp(BF16), shp(BF16), shp(F32), shp(F32), shp(BF16), shp(BF16), shp(F32)]
        + ([shp(F32)] if want_cvn else []),
        scratch_shapes=[
            pltpu.VMEM((m, D_BRANCH), BF16),
            pltpu.VMEM((m, D_BRANCH), F32),
        ],
        compiler_params=pltpu.CompilerParams(
            dimension_semantics=("arbitrary", "arbitrary"),
            vmem_limit_bytes=VMEM_LIMIT_BYTES),
        name="cd_proj",
    )(h, g, w_in, ln_g, ln_b, ws, cbt)


def _dot_nt(a, b):
    return lax.dot_general(a, b, (((1,), (1,)), ((), ())), preferred_element_type=F32)


def _pair_lanes(pr):
    return slice(pr * LANES, (pr + 1) * LANES)


def _sb_setup(q, qm_s, tri_s):
    tq = q.shape[0]
    lane = lax.broadcasted_iota(jnp.int32, (tq, LANES), 1)
    for hd in range(D_HEADS):
        pair = q[:, _pair_lanes(hd // 2)]
        in_head = (lane >= D_HEAD_DIM) if hd % 2 else (lane < D_HEAD_DIM)
        qm_s[hd * tq:(hd + 1) * tq, :] = jnp.where(in_head, pair, jnp.zeros_like(pair))
    n = tri_s.shape[1]
    row = lax.broadcasted_iota(jnp.int32, (n, n), 0)
    col = lax.broadcasted_iota(jnp.int32, (n, n), 1)
    tri = jnp.where(row > col, 1.0, 0.0).astype(BF16)
    tri_s[0:n, :] = tri
    tri_s[n:2 * n, :] = tri


def _sb_diag_mask(tq):
    shape = (2 * tq, KEY_BLOCK)
    r = lax.broadcasted_iota(jnp.int32, shape, 0)
    qi = jnp.where(r >= tq, r - tq, r)
    return lax.broadcasted_iota(jnp.int32, shape, 1) < qi


def _sb_pass(qk, av, nk, diag_valid, first, qm_s, acc_s, car_s, tri_s, lb_s, hl_s, a_s):
    m2 = qm_s.shape[0] // (D_HEADS // 2)
    pairs = [pl.ds(pr * m2, m2) for pr in range(D_HEADS // 2)]
    tri2 = jnp.concatenate([tri_s[0:nk, 0:nk], tri_s[2 * KEY_BLOCK:2 * KEY_BLOCK + nk, 0:nk]], axis=0)

    def masked(x):
        if diag_valid is None:
            return x
        return jnp.concatenate(
            [x[:, :nk - KEY_BLOCK], jnp.where(diag_valid, x[:, nk - KEY_BLOCK:], 0.0)], axis=1)

    for pr, rows in enumerate(pairs):
        z = qk(pr, qm_s[rows, :])
        soft = jnp.log(1.0 + jnp.exp(-jnp.abs(z)))
        log_beta = jnp.minimum(z, 0.0) - soft
        log_keep = masked(log_beta - z)
        hi = log_keep.astype(BF16)
        hl_s[rows, 0:nk] = hi
        hl_s[rows, nk:2 * nk] = (log_keep - hi.astype(F32)).astype(BF16)
        total = jnp.sum(log_keep, axis=-1, keepdims=True)
        if first:
            lb_s[rows, 0:nk] = log_beta
            car_s[rows, :] = jnp.broadcast_to(total, (m2, LANES))
        else:
            carry = car_s[rows, :]
            lb_s[rows, 0:nk] = log_beta + jnp.concatenate([carry] * (nk // LANES), axis=1)
            car_s[rows, :] = carry + total
    for pr, rows in enumerate(pairs):
        later = _dot(hl_s[rows, 0:2 * nk], tri2)
        a_s[rows, 0:nk] = masked(jnp.exp(lb_s[rows, 0:nk] + later)).astype(BF16)
    for pr, rows in enumerate(pairs):
        pv = av(pr, a_s[rows, 0:nk])
        acc_s[rows, :] = pv if first else acc_s[rows, :] + pv


def _sb_all_spent(car_s):
    return (jnp.max(car_s[...]) < STICK_LOG_FLOOR).astype(jnp.int32)


def _sb_sweep(first_block, tail_pass, car_s):
    def cond(c):
        j, spent = c
        return jnp.logical_and(j >= 0, spent == 0)

    def body(c):
        j, _ = c
        tail_pass(j)
        return j - 1, _sb_all_spent(car_s)

    lax.while_loop(cond, body, (jnp.asarray(first_block, jnp.int32), _sb_all_spent(car_s)))


def _sb_finish(o_ref, acc_s):
    tq = acc_s.shape[0] // D_HEADS
    lane = lax.broadcasted_iota(jnp.int32, (tq, LANES), 1)
    for pr in range(D_HEADS // 2):
        even = acc_s[(2 * pr) * tq:(2 * pr + 1) * tq, :]
        odd = acc_s[(2 * pr + 1) * tq:(2 * pr + 2) * tq, :]
        o_ref[0, :, _pair_lanes(pr)] = jnp.where(lane < D_HEAD_DIM, even, odd)


def _key_rows(j):
    return pl.ds(pl.multiple_of(j * KEY_BLOCK, KEY_BLOCK), KEY_BLOCK)


def _sb_self_kernel(q_ref, k_ref, v_ref, o_ref, qm_s, acc_s, car_s, tri_s, lb_s, hl_s, a_s):
    i = pl.program_id(1)
    tq = q_ref.shape[1]
    scratch = (qm_s, acc_s, car_s, tri_s, lb_s, hl_s, a_s)
    _sb_setup(q_ref[0], qm_s, tri_s)
    prev, diag = _key_rows(jnp.maximum(i - 1, 0)), _key_rows(i)

    def keys(pr):
        return jnp.concatenate([k_ref[0, prev, _pair_lanes(pr)], k_ref[0, diag, _pair_lanes(pr)]], axis=0)

    def values(pr):
        v_prev = v_ref[0, prev, _pair_lanes(pr)]
        v_prev = jnp.where(jnp.broadcast_to(i, v_prev.shape) > 0, v_prev, jnp.zeros_like(v_prev))
        return jnp.concatenate([v_prev, v_ref[0, diag, _pair_lanes(pr)]], axis=0)

    _sb_pass(lambda pr, q: _dot_nt(q, keys(pr)), lambda pr, a: _dot(a, values(pr)),
             2 * KEY_BLOCK, _sb_diag_mask(tq), True, *scratch)

    def tail_pass(j):
        rows = _key_rows(j)
        _sb_pass(lambda pr, q: _dot_nt(q, k_ref[0, rows, _pair_lanes(pr)]),
                 lambda pr, a: _dot(a, v_ref[0, rows, _pair_lanes(pr)]),
                 KEY_BLOCK, None, False, *scratch)

    _sb_sweep(i - 2, tail_pass, car_s)
    _sb_finish(o_ref, acc_s)


def _sb_cached_kernel(q_ref, kn_ref, vn_ref, kl_ref, vl_ref, kc_hbm, vc_hbm, o_ref,
                      qm_s, acc_s, car_s, tri_s, lb_s, hl_s, a_s, kd_s, vd_s, kt_s, vt_s,
                      *, n_past_blocks):
    b = pl.program_id(0)
    tq = q_ref.shape[1]
    tn = kn_ref.shape[1]
    scratch = (qm_s, acc_s, car_s, tri_s, lb_s, hl_s, a_s)
    _sb_setup(q_ref[0], qm_s, tri_s)
    kd_s[...] = jnp.zeros_like(kd_s)
    vd_s[...] = jnp.zeros_like(vd_s)
    kd_s[0:tn, :] = kn_ref[0]
    vd_s[0:tn, :] = vn_ref[0]

    def pair_t(ref, pr):
        return ref[2 * pr:2 * pr + 2].reshape(LANES, KEY_BLOCK).astype(BF16)

    def qk_first(pr, q):
        return jnp.concatenate([_dot(q, pair_t(kl_ref.at[0], pr)), _dot_nt(q, kd_s[:, _pair_lanes(pr)])], axis=1)

    def av_first(pr, a):
        return _dot_nt(a[:, :KEY_BLOCK], pair_t(vl_ref.at[0], pr)) + _dot(a[:, KEY_BLOCK:], vd_s[:, _pair_lanes(pr)])

    _sb_pass(qk_first, av_first, 2 * KEY_BLOCK, _sb_diag_mask(tq), True, *scratch)

    def tail_pass(j):
        cols = _key_rows(j)
        pltpu.sync_copy(kc_hbm.at[b, :, :, cols], kt_s)
        pltpu.sync_copy(vc_hbm.at[b, :, :, cols], vt_s)
        _sb_pass(lambda pr, q: _dot(q, pair_t(kt_s, pr)), lambda pr, a: _dot_nt(a, pair_t(vt_s, pr)),
                 KEY_BLOCK, None, False, *scratch)

    _sb_sweep(n_past_blocks - 2, tail_pass, car_s)
    _sb_finish(o_ref, acc_s)


def _sb_scratch(tq):
    rows = D_HEADS * tq
    return [
        pltpu.VMEM((rows, LANES), BF16),
        pltpu.VMEM((rows, LANES), F32),
        pltpu.VMEM((rows, LANES), F32),
        pltpu.VMEM((4 * KEY_BLOCK, 2 * KEY_BLOCK), BF16),
        pltpu.VMEM((rows, 2 * KEY_BLOCK), F32),
        pltpu.VMEM((rows, 4 * KEY_BLOCK), BF16),
        pltpu.VMEM((rows, 2 * KEY_BLOCK), BF16),
    ]


def _sb_self(q, kb, vb):
    nb, t, _ = q.shape
    assert t % KEY_BLOCK == 0
    full = pl.BlockSpec((1, t, D_BRANCH), lambda b, i: (b, 0, 0))
    blk = pl.BlockSpec((1, KEY_BLOCK, D_BRANCH), lambda b, i: (b, i, 0))
    return pl.pallas_call(
        _sb_self_kernel,
        grid=(nb, t // KEY_BLOCK),
        in_specs=[blk, full, full],
        out_specs=blk,
        out_shape=jax.ShapeDtypeStruct((nb, t, D_BRANCH), F32),
        scratch_shapes=_sb_scratch(KEY_BLOCK),
        compiler_params=pltpu.CompilerParams(
            dimension_semantics=("arbitrary", "arbitrary"),
            vmem_limit_bytes=VMEM_LIMIT_BYTES),
        name="sb_self",
    )(q, kb, vb)


def _sb_cached(q, kb_new, vb_new, k_cache, v_cache):
    nb, tq, _ = q.shape
    t_past = k_cache.shape[-1]
    assert tq <= KEY_BLOCK and tq % (2 * SUBLANES) == 0 and t_past % KEY_BLOCK == 0
    n_past_blocks = t_past // KEY_BLOCK
    new = pl.BlockSpec((1, tq, D_BRANCH), lambda b: (b, 0, 0))
    last = pl.BlockSpec((1, D_HEADS, D_HEAD_DIM, KEY_BLOCK), lambda b: (b, 0, 0, n_past_blocks - 1))
    anywhere = pl.BlockSpec(memory_space=pl.ANY)
    cache_blk = pltpu.VMEM((D_HEADS, D_HEAD_DIM, KEY_BLOCK), F32)
    return pl.pallas_call(
        functools.partial(_sb_cached_kernel, n_past_blocks=n_past_blocks),
        grid=(nb,),
        in_specs=[new, new, new, last, last, anywhere, anywhere],
        out_specs=new,
        out_shape=jax.ShapeDtypeStruct((nb, tq, D_BRANCH), F32),
        scratch_shapes=_sb_scratch(tq) + [
            pltpu.VMEM((KEY_BLOCK, D_BRANCH), BF16),
            pltpu.VMEM((KEY_BLOCK, D_BRANCH), BF16),
            cache_blk, cache_blk,
        ],
        compiler_params=pltpu.CompilerParams(
            dimension_semantics=("arbitrary",),
            vmem_limit_bytes=VMEM_LIMIT_BYTES),
        name="sb_cached",
    )(q, kb_new, vb_new, k_cache, v_cache, k_cache, v_cache)


def _cd_out_kernel(h_ref, cout_ref, o_ref, sdz_ref, p_ref, wout_ref, gate_ref, pproj_ref, fg_ref,
                   y_ref, *, bb, tt):
    m = bb * tt
    x = h_ref[...].reshape(m, D_MODEL)
    d_out = (o_ref[...] * sdz_ref[...]).reshape(m, D_BRANCH).astype(BF16)
    c_out = cout_ref[...].reshape(m, D_BRANCH)
    h = x + _dot(c_out, wout_ref[0:D_BRANCH, :]) + _dot(d_out, wout_ref[D_BRANCH:, :])
    gate = jax.nn.sigmoid(_dot(h.astype(BF16), gate_ref[...]))
    emb = _dot(p_ref[...].reshape(m, PLE_DIM).astype(BF16), pproj_ref[...])
    y_ref[...] = _rms(h + gate * emb, fg_ref[...]).reshape(bb, tt, D_MODEL)


def _cd_out(h, cout, o, sdz, p_all, layer, w_out, gate, pproj, fg, *, bb, tt):
    nb, t, _ = h.shape
    grid = (nb // bb, t // tt)
    tok = lambda d: pl.BlockSpec((bb, tt, d), lambda b, s: (b, s, 0))
    return pl.pallas_call(
        functools.partial(_cd_out_kernel, bb=bb, tt=tt),
        grid=grid,
        in_specs=[
            tok(D_MODEL), tok(D_BRANCH), tok(D_BRANCH), tok(D_BRANCH),
            pl.BlockSpec((None, bb, tt, PLE_DIM), lambda b, s: (layer, b, s, 0)),
            _const_spec((2 * D_BRANCH, D_MODEL)),
            _const_spec((D_MODEL, D_MODEL)),
            _const_spec((PLE_DIM, D_MODEL)),
            _const_spec((1, D_MODEL)),
        ],
        out_specs=tok(D_MODEL),
        out_shape=jax.ShapeDtypeStruct((nb, t, D_MODEL), F32),
        compiler_params=pltpu.CompilerParams(
            dimension_semantics=("arbitrary", "arbitrary"),
            vmem_limit_bytes=VMEM_LIMIT_BYTES),
        name="cd_out",
    )(h, cout, o, sdz, p_all, w_out, gate, pproj, fg)


def _trunk(x, p_all, hist_a, hist_b, k_cache, v_cache, w, *, bb, tt, n_sub):
    t = x.shape[1]
    chunk = min(t, C_CHUNK)
    h1, new_a, new_b = _ab_layer(
        x, p_all, 0, hist_a, hist_b, w["norm_g"][0], w["ab_w_in"], w["a_conv_w"], w["b_conv_w"],
        w["b_ln_g"], w["b_ln_b"], w["ab_w_out"], w["ple_gate"][0], w["ple_proj"][0],
        bb=bb, tt=tt, n_sub=n_sub)
    tt1 = tt
    cout, q, k, v, kb, vb, sdz, *cvn = _cd_proj(
        h1, w["norm_g"][1], w["cd_w_in"], w["c_ln_g"], w["c_ln_b"],
        w["c_ws"][:, :chunk, :chunk], w["c_bt"][:chunk], bb=bb, tt=tt1, chunk=chunk,
        want_cvn=k_cache is not None)
    if k_cache is None:
        o = _sb_self(q, kb, vb)
    else:
        o = _sb_cached(q, kb, vb, k_cache, v_cache)
    y = _cd_out(h1, cout, o, sdz, p_all, 1, w["cd_w_out"], w["ple_gate"][1], w["ple_proj"][1],
                w["final_g"], bb=bb, tt=tt1)
    return y, new_a, new_b, (cvn[0] if cvn else None), k, v


def kernel(x_prompt, x_sample, state_a_conv, state_b_conv, cache_d_k, cache_d_v, p_prompt, p_sample,
           norm_g, ple_gate, ple_proj, ab_w_in, a_conv_w, b_conv_w, b_ln_g, b_ln_b, ab_w_out,
           cd_w_in, c_ln_g, c_ln_b, c_ws, c_b, cd_w_out, final_g):
    bp, tp, _ = x_prompt.shape
    bs, ts, _ = x_sample.shape
    w = {
        "norm_g": norm_g.reshape(2, 1, D_MODEL),
        "ple_gate": ple_gate.astype(BF16),
        "ple_proj": ple_proj.astype(BF16),
        "ab_w_in": ab_w_in[0].astype(BF16),
        "a_conv_w": a_conv_w[0],
        "b_conv_w": b_conv_w[0],
        "b_ln_g": b_ln_g, "b_ln_b": b_ln_b,
        "ab_w_out": ab_w_out[0].astype(BF16),
        "cd_w_in": cd_w_in[0].astype(BF16),
        "c_ln_g": c_ln_g, "c_ln_b": c_ln_b,
        "c_ws": c_ws[0],
        "c_bt": c_b[0].T,
        "cd_w_out": cd_w_out[0].astype(BF16),
        "final_g": final_g.reshape(1, D_MODEL),
    }
    zeros_a = jnp.zeros((bp, HIST_A, D_BRANCH), F32)
    zeros_b = jnp.zeros((bp, HIST_B, D_BRANCH), F32)
    yp, pa, pb, _, pk, pv = _trunk(x_prompt, p_prompt, zeros_a, zeros_b, None, None, w,
                                   bb=1, tt=512, n_sub=2)
    k_cache = jnp.transpose(cache_d_k[0], (0, 2, 3, 1))
    v_cache = jnp.transpose(cache_d_v[0], (0, 2, 3, 1))
    ys, sa, sb, scv, sk, sv = _trunk(x_sample, p_sample, state_a_conv[0], state_b_conv[0],
                                     k_cache, v_cache, w, bb=8, tt=ts, n_sub=1)
    heads = lambda a: a.reshape(1, a.shape[0], a.shape[1], D_HEADS, D_HEAD_DIM)
    return (yp, ys, pa[None], sa[None], pb[None], sb[None], scv[None],
            heads(pk), heads(pv), heads(sk), heads(sv))
```

```python
import functools

import jax
import jax.numpy as jnp
from jax import lax
from jax.experimental import pallas as pl
from jax.experimental.pallas import tpu as pltpu

D_MODEL = 1024
PLE_DIM = 256
D_BRANCH = 512
CONV_A = 3
CONV_B = 31
C_HEADS = 4
C_CHUNK = 128
C_HEAD_DIM = D_BRANCH // C_HEADS
D_HEADS = 8
D_HEAD_DIM = D_BRANCH // D_HEADS
EPS = 1e-6

LANES = 128
SUBLANES = 8
VMEM_LIMIT_BYTES = 56 * 1024 * 1024

HIST_A = CONV_A - 1
HIST_B = CONV_B - 1
PAD_A = SUBLANES
PAD_B = 4 * SUBLANES
CONV_ROWS = 32
KEY_BLOCK = 128
STICK_LOG_FLOOR = -88.0
MASKED_SCORE = -1.0e4

F32 = jnp.float32
BF16 = jnp.bfloat16


def _rms(x, g):
    return x * lax.rsqrt(jnp.mean(x * x, axis=-1, keepdims=True) + EPS) * g


def _layernorm(x, g, b):
    mu = jnp.mean(x, axis=-1, keepdims=True)
    xc = x - mu
    return xc * lax.rsqrt(jnp.mean(xc * xc, axis=-1, keepdims=True) + EPS) * g + b


def _silu(x):
    return x * jax.nn.sigmoid(x)


def _dot(a, b):
    return jnp.dot(a, b, preferred_element_type=F32)


def _ab_layer_kernel(x_ref, p_ref, ha_ref, hb_ref, g_ref, win_ref, aw_ref, bw_ref,
                     lng_ref, lnb_ref, wout_ref, gate_ref, pproj_ref,
                     h_ref, na_ref, nb_ref,
                     sh_s, wa_s, wb_s, *bufs, bb, tt, n_sub):
    s = pl.program_id(1)
    ts = tt // n_sub
    msub = bb * ts
    ua, ub, pr, ab, xb, hs = (bufs[i * n_sub:(i + 1) * n_sub] for i in range(6))
    hist_a = slice(PAD_A - HIST_A, PAD_A)
    hist_b = slice(PAD_B - HIST_B, PAD_B)
    tail_a = slice(PAD_A + ts - HIST_A, PAD_A + ts)
    tail_b = slice(PAD_B + ts - HIST_B, PAD_B + ts)

    @pl.when(s == 0)
    def _load_history():
        ua[0][:, hist_a, :] = ha_ref[...]
        ub[0][:, hist_b, :] = hb_ref[...]

    for j in range(CONV_A):
        wa_s[j] = jnp.broadcast_to(aw_ref[j:j + 1, :], (SUBLANES, D_BRANCH))
    for j in range(CONV_B):
        wb_s[j] = jnp.broadcast_to(bw_ref[j:j + 1, :], (SUBLANES, D_BRANCH))

    def rows_of(tile):
        return jnp.concatenate([tile] * (CONV_ROWS // SUBLANES), axis=0)

    def col(i):
        return slice(i * D_BRANCH, (i + 1) * D_BRANCH)

    def project_jobs(k):
        t0 = k * ts

        def normalise():
            x = x_ref[:, t0:t0 + ts, :].reshape(msub, D_MODEL)
            xb[k][...] = _rms(x, g_ref[...]).astype(BF16)

        def proj(i):
            pr[k][:, col(i)] = _dot(xb[k][...], win_ref[:, col(i)])

        return [normalise] + [functools.partial(proj, i) for i in range(7)]

    def conv_jobs(k):
        def conv_inputs():
            ua[k][:, PAD_A:, :] = (pr[k][:, col(1)] * pr[k][:, col(0)]).reshape(bb, ts, D_BRANCH)
            ub[k][:, PAD_B:, :] = (
                pr[k][:, col(4)] * jax.nn.sigmoid(pr[k][:, col(5)])).reshape(bb, ts, D_BRANCH)
            if k > 0:
                ua[k][:, hist_a, :] = ua[k - 1][:, tail_a, :]
                ub[k][:, hist_b, :] = ub[k - 1][:, tail_b, :]

        def shifted_copies(b):
            for shift in range(1, SUBLANES):
                sh_s[shift - 1] = ub[k][b, shift:shift + PAD_B + ts - SUBLANES, :]

        def chunk(b, r0):
            if b == 0 and r0 == 0:
                conv_inputs()
            if r0 == 0:
                shifted_copies(b)
            rows = pl.ds(b * ts + r0, CONV_ROWS)
            conv_a = None
            for j in range(CONV_A):
                term = rows_of(wa_s[j]) * ua[k][b, pl.ds(PAD_A - HIST_A + r0 + j, CONV_ROWS), :]
                conv_a = term if conv_a is None else conv_a + term
            ab[k][rows, 0:D_BRANCH] = (pr[k][rows, col(2)] * _silu(pr[k][rows, col(3)]) * conv_a).astype(BF16)

            conv_b = None
            for j in range(CONV_B):
                blk, shift = divmod(PAD_B - HIST_B + j, SUBLANES)
                src = pl.ds(r0 + blk * SUBLANES, CONV_ROWS)
                tap = ub[k][b, src, :] if shift == 0 else sh_s[shift - 1, src, :]
                term = rows_of(wb_s[j]) * tap
                conv_b = term if conv_b is None else conv_b + term
            b_act = _silu(_layernorm(conv_b, lng_ref[...], lnb_ref[...]))
            ab[k][rows, D_BRANCH:2 * D_BRANCH] = (b_act * _silu(pr[k][rows, col(6)])).astype(BF16)

        return [functools.partial(chunk, b, r0) for b in range(bb) for r0 in range(0, ts, CONV_ROWS)]

    def finish_jobs(k):
        t0 = k * ts

        def residual():
            x = x_ref[:, t0:t0 + ts, :].reshape(msub, D_MODEL)
            hs[k][...] = x + _dot(ab[k][...], wout_ref[...])

        def embed():
            h = hs[k][...]
            gate = jax.nn.sigmoid(_dot(h.astype(BF16), gate_ref[...]))
            emb = _dot(p_ref[:, t0:t0 + ts, :].reshape(msub, PLE_DIM).astype(BF16), pproj_ref[...])
            h_ref[:, t0:t0 + ts, :] = (h + gate * emb).reshape(bb, ts, D_MODEL)

        return [residual, embed]

    def interleave(vpu_jobs, mxu_jobs):
        nv, nm = len(vpu_jobs), len(mxu_jobs)
        done = 0
        for i, job in enumerate(vpu_jobs):
            while done < nm and done * nv <= i * nm:
                mxu_jobs[done]()
                done += 1
            job()
        for job in mxu_jobs[done:]:
            job()

    interleave([], project_jobs(0))
    for k in range(n_sub):
        mxu_jobs = (finish_jobs(k - 1) if k > 0 else []) + (project_jobs(k + 1) if k + 1 < n_sub else [])
        interleave(conv_jobs(k), mxu_jobs)
    interleave([], finish_jobs(n_sub - 1))

    last_a = ua[n_sub - 1][:, tail_a, :]
    last_b = ub[n_sub - 1][:, tail_b, :]
    ua[0][:, hist_a, :] = last_a
    ub[0][:, hist_b, :] = last_b

    @pl.when(s == pl.num_programs(1) - 1)
    def _store_state():
        na_ref[...] = last_a
        nb_ref[...] = last_b


def _const_spec(shape):
    return pl.BlockSpec(shape, lambda b, s: (0,) * len(shape))


def _ab_layer(x, p_all, layer, hist_a, hist_b, g, w_in, a_w, b_w, ln_g, ln_b, w_out, gate, pproj,
              *, bb, tt, n_sub):
    nb, t, _ = x.shape
    assert nb % bb == 0 and t % tt == 0 and tt % n_sub == 0 and (bb == 1 or n_sub == 1)
    ts = tt // n_sub
    assert ts % CONV_ROWS == 0 and ts >= HIST_B
    msub = bb * ts
    grid = (nb // bb, t // tt)
    tok = lambda d: pl.BlockSpec((bb, tt, d), lambda b, s: (b, s, 0))
    state = lambda r: pl.BlockSpec((bb, r, D_BRANCH), lambda b, s: (b, 0, 0))
    return pl.pallas_call(
        functools.partial(_ab_layer_kernel, bb=bb, tt=tt, n_sub=n_sub),
        grid=grid,
        in_specs=[
            tok(D_MODEL),
            pl.BlockSpec((None, bb, tt, PLE_DIM), lambda b, s: (layer, b, s, 0)),
            state(HIST_A), state(HIST_B),
            _const_spec((1, D_MODEL)),
            _const_spec((D_MODEL, 7 * D_BRANCH)),
            _const_spec((CONV_A, D_BRANCH)),
            _const_spec((CONV_B, D_BRANCH)),
            _const_spec((1, D_BRANCH)), _const_spec((1, D_BRANCH)),
            _const_spec((2 * D_BRANCH, D_MODEL)),
            _const_spec((D_MODEL, D_MODEL)),
            _const_spec((PLE_DIM, D_MODEL)),
        ],
        out_specs=[tok(D_MODEL), state(HIST_A), state(HIST_B)],
        out_shape=[
            jax.ShapeDtypeStruct((nb, t, D_MODEL), F32),
            jax.ShapeDtypeStruct((nb, HIST_A, D_BRANCH), F32),
            jax.ShapeDtypeStruct((nb, HIST_B, D_BRANCH), F32),
        ],
        scratch_shapes=[
            pltpu.VMEM((SUBLANES - 1, PAD_B + ts - SUBLANES, D_BRANCH), F32),
            pltpu.VMEM((CONV_A, SUBLANES, D_BRANCH), F32),
            pltpu.VMEM((CONV_B, SUBLANES, D_BRANCH), F32),
        ] + [
            pltpu.VMEM(shape, dtype)
            for shape, dtype in [
                ((bb, PAD_A + ts, D_BRANCH), F32),
                ((bb, PAD_B + ts, D_BRANCH), F32),
                ((msub, 7 * D_BRANCH), F32),
                ((msub, 2 * D_BRANCH), BF16),
                ((msub, D_MODEL), BF16),
                ((msub, D_MODEL), F32),
            ]
            for _ in range(n_sub)
        ],
        compiler_params=pltpu.CompilerParams(
            dimension_semantics=("arbitrary", "arbitrary"),
            vmem_limit_bytes=VMEM_LIMIT_BYTES),
        name="ab_layer",
    )(x, p_all, hist_a, hist_b, g, w_in, a_w, b_w, ln_g, ln_b, w_out, gate, pproj)


def _cd_proj_kernel(h_ref, g_ref, win_ref, lng_ref, lnb_ref, ws_ref, cbt_ref,
                    cout_ref, q_ref, k_ref, v_ref, kb_ref, vb_ref, sdz_ref, *rest, bb, tt, chunk):
    cvn_s, gc_s = rest[-2:]
    m = bb * tt
    x = h_ref[...].reshape(m, D_MODEL)
    xb = _rms(x, g_ref[...]).astype(BF16)

    def proj(i):
        return _dot(xb, win_ref[:, i * D_BRANCH:(i + 1) * D_BRANCH])

    gc_s[...] = proj(0) * _silu(proj(2))
    cvn = _layernorm(proj(1), lng_ref[...], lnb_ref[...])
    if len(rest) == 3:
        rest[0][...] = cvn.reshape(bb, tt, D_BRANCH)
    cvn_s[...] = cvn.astype(BF16)
    q_ref[...] = (proj(3) * (D_HEAD_DIM ** -0.5)).astype(BF16).reshape(bb, tt, D_BRANCH)
    k = proj(4).reshape(bb, tt, D_BRANCH)
    k_ref[...] = k
    kb_ref[...] = k.astype(BF16)
    v = proj(5).reshape(bb, tt, D_BRANCH)
    v_ref[...] = v
    vb_ref[...] = v.astype(BF16)
    sdz_ref[...] = _silu(proj(6)).reshape(bb, tt, D_BRANCH)

    row = lax.broadcasted_iota(jnp.int32, (chunk, chunk), 0)
    col = lax.broadcasted_iota(jnp.int32, (chunk, chunk), 1)
    for hd in range(C_HEADS):
        lanes = slice(hd * C_HEAD_DIM, (hd + 1) * C_HEAD_DIM)
        w_tril = jnp.where(col <= row, ws_ref[hd], 0.0).astype(BF16)
        bias = cbt_ref[:, hd:hd + 1]
        for c0 in range(0, m, chunk):
            rows = pl.ds(c0, chunk)
            mixed = _dot(w_tril, cvn_s[rows, lanes]) + bias
            cout_ref[c0 // tt, pl.ds(c0 % tt, chunk), lanes] = (gc_s[rows, lanes] * mixed).astype(BF16)


def _cd_proj(h, g, w_in, ln_g, ln_b, ws, cbt, *, bb, tt, chunk, want_cvn):
    nb, t, _ = h.shape
    assert nb % bb == 0 and t % tt == 0 and tt % chunk == 0
    m = bb * tt
    grid = (nb // bb, t // tt)
    tok = lambda d: pl.BlockSpec((bb, tt, d), lambda b, s: (b, s, 0))
    shp = lambda dt: jax.ShapeDtypeStruct((nb, t, D_BRANCH), dt)
    return pl.pallas_call(
        functools.partial(_cd_proj_kernel, bb=bb, tt=tt, chunk=chunk),
        grid=grid,
        in_specs=[
            tok(D_MODEL),
            _const_spec((1, D_MODEL)),
            _const_spec((D_MODEL, 7 * D_BRANCH)),
            _const_spec((1, D_BRANCH)), _const_spec((1, D_BRANCH)),
            _const_spec((C_HEADS, chunk, chunk)),
            _const_spec((chunk, C_HEADS)),
        ],
        out_specs=[tok(D_BRANCH)] * (8 if want_cvn else 7),
        out_shape=[shp(BF16), shp(BF16), shp(F32), shp(F32), shp(BF16), shp(BF16), shp(F32)]
        + ([shp(F32)] if want_cvn else []),
        scratch_shapes=[
            pltpu.VMEM((m, D_BRANCH), BF16),
            pltpu.VMEM((m, D_BRANCH), F32),
        ],
        compiler_params=pltpu.CompilerParams(
            dimension_semantics=("arbitrary", "arbitrary"),
            vmem_limit_bytes=VMEM_LIMIT_BYTES),
        name="cd_proj",
    )(h, g, w_in, ln_g, ln_b, ws, cbt)


def _dot_nt(a, b):
    return lax.dot_general(a, b, (((1,), (1,)), ((), ())), preferred_element_type=F32)


def _pair_lanes(pr):
    return slice(pr * LANES, (pr + 1) * LANES)


def _sb_setup(q, qm_s, tri_s):
    tq = q.shape[0]
    lane = lax.broadcasted_iota(jnp.int32, (tq, LANES), 1)
    for hd in range(D_HEADS):
        pair = q[:, _pair_lanes(hd // 2)]
        in_head = (lane >= D_HEAD_DIM) if hd % 2 else (lane < D_HEAD_DIM)
        qm_s[hd * tq:(hd + 1) * tq, :] = jnp.where(in_head, pair, jnp.zeros_like(pair))
    n = tri_s.shape[1]
    row = lax.broadcasted_iota(jnp.int32, (n, n), 0)
    col = lax.broadcasted_iota(jnp.int32, (n, n), 1)
    tri = jnp.where(row > col, 1.0, 0.0).astype(BF16)
    tri_s[0:n, :] = tri
    tri_s[n:2 * n, :] = tri


def _sb_diag_mask(tq):
    shape = (2 * tq, KEY_BLOCK)
    r = lax.broadcasted_iota(jnp.int32, shape, 0)
    qi = jnp.where(r >= tq, r - tq, r)
    return lax.broadcasted_iota(jnp.int32, shape, 1) < qi


def _sb_pass(qk, av, nk, diag_valid, first, qm_s, acc_s, car_s, tri_s, lb_s, hl_s, a_s):
    m2 = qm_s.shape[0] // (D_HEADS // 2)
    pairs = [pl.ds(pr * m2, m2) for pr in range(D_HEADS // 2)]
    tri2 = jnp.concatenate([tri_s[0:nk, 0:nk], tri_s[2 * KEY_BLOCK:2 * KEY_BLOCK + nk, 0:nk]], axis=0)

    def masked(z):
        if diag_valid is None:
            return z
        return jnp.concatenate(
            [z[:, :nk - KEY_BLOCK], jnp.where(diag_valid, z[:, nk - KEY_BLOCK:], MASKED_SCORE)], axis=1)

    for pr, rows in enumerate(pairs):
        z = masked(qk(pr, qm_s[rows, :]))
        soft = jnp.log(1.0 + jnp.exp(-jnp.abs(z)))
        log_beta = jnp.minimum(z, 0.0) - soft
        log_keep = log_beta - z
        hi = log_keep.astype(BF16)
        hl_s[rows, 0:nk] = hi
        hl_s[rows, nk:2 * nk] = (log_keep - hi.astype(F32)).astype(BF16)
        total = jnp.sum(log_keep, axis=-1, keepdims=True)
        if first:
            lb_s[rows, 0:nk] = log_beta
            car_s[rows, :] = jnp.broadcast_to(total, (m2, LANES))
        else:
            carry = car_s[rows, :]
            lb_s[rows, 0:nk] = log_beta + jnp.concatenate([carry] * (nk // LANES), axis=1)
            car_s[rows, :] = carry + total
    for pr, rows in enumerate(pairs):
        later = _dot(hl_s[rows, 0:2 * nk], tri2)
        a_s[rows, 0:nk] = jnp.exp(lb_s[rows, 0:nk] + later).astype(BF16)
    for pr, rows in enumerate(pairs):
        pv = av(pr, a_s[rows, 0:nk])
        acc_s[rows, :] = pv if first else acc_s[rows, :] + pv


def _sb_all_spent(car_s):
    return (jnp.max(car_s[...]) < STICK_LOG_FLOOR).astype(jnp.int32)


def _sb_sweep(first_block, tail_pass, car_s):
    def cond(c):
        j, spent = c
        return jnp.logical_and(j >= 0, spent == 0)

    def body(c):
        j, _ = c
        tail_pass(j)
        return j - 1, _sb_all_spent(car_s)

    lax.while_loop(cond, body, (jnp.asarray(first_block, jnp.int32), _sb_all_spent(car_s)))


def _sb_finish(o_ref, acc_s):
    tq = acc_s.shape[0] // D_HEADS
    lane = lax.broadcasted_iota(jnp.int32, (tq, LANES), 1)
    for pr in range(D_HEADS // 2):
        even = acc_s[(2 * pr) * tq:(2 * pr + 1) * tq, :]
        odd = acc_s[(2 * pr + 1) * tq:(2 * pr + 2) * tq, :]
        o_ref[0, :, _pair_lanes(pr)] = jnp.where(lane < D_HEAD_DIM, even, odd)


def _key_rows(j):
    return pl.ds(pl.multiple_of(j * KEY_BLOCK, KEY_BLOCK), KEY_BLOCK)


def _sb_self_kernel(q_ref, k_ref, v_ref, o_ref, qm_s, acc_s, car_s, tri_s, lb_s, hl_s, a_s):
    i = pl.program_id(1)
    tq = q_ref.shape[1]
    scratch = (qm_s, acc_s, car_s, tri_s, lb_s, hl_s, a_s)
    _sb_setup(q_ref[0], qm_s, tri_s)
    prev, diag = _key_rows(jnp.maximum(i - 1, 0)), _key_rows(i)

    def keys(pr):
        return jnp.concatenate([k_ref[0, prev, _pair_lanes(pr)], k_ref[0, diag, _pair_lanes(pr)]], axis=0)

    def values(pr):
        v_prev = v_ref[0, prev, _pair_lanes(pr)]
        v_prev = jnp.where(jnp.broadcast_to(i, v_prev.shape) > 0, v_prev, jnp.zeros_like(v_prev))
        return jnp.concatenate([v_prev, v_ref[0, diag, _pair_lanes(pr)]], axis=0)

    _sb_pass(lambda pr, q: _dot_nt(q, keys(pr)), lambda pr, a: _dot(a, values(pr)),
             2 * KEY_BLOCK, _sb_diag_mask(tq), True, *scratch)

    def tail_pass(j):
        rows = _key_rows(j)
        _sb_pass(lambda pr, q: _dot_nt(q, k_ref[0, rows, _pair_lanes(pr)]),
                 lambda pr, a: _dot(a, v_ref[0, rows, _pair_lanes(pr)]),
                 KEY_BLOCK, None, False, *scratch)

    _sb_sweep(i - 2, tail_pass, car_s)
    _sb_finish(o_ref, acc_s)


def _sb_cached_kernel(q_ref, kn_ref, vn_ref, kl_ref, vl_ref, kc_hbm, vc_hbm, o_ref,
                      qm_s, acc_s, car_s, tri_s, lb_s, hl_s, a_s, kd_s, vd_s, kt_s, vt_s,
                      *, n_past_blocks):
    b = pl.program_id(0)
    tq = q_ref.shape[1]
    tn = kn_ref.shape[1]
    scratch = (qm_s, acc_s, car_s, tri_s, lb_s, hl_s, a_s)
    _sb_setup(q_ref[0], qm_s, tri_s)
    kd_s[...] = jnp.zeros_like(kd_s)
    vd_s[...] = jnp.zeros_like(vd_s)
    kd_s[0:tn, :] = kn_ref[0]
    vd_s[0:tn, :] = vn_ref[0]

    def pair_t(ref, pr):
        return ref[2 * pr:2 * pr + 2].reshape(LANES, KEY_BLOCK).astype(BF16)

    def qk_first(pr, q):
        return jnp.concatenate([_dot(q, pair_t(kl_ref.at[0], pr)), _dot_nt(q, kd_s[:, _pair_lanes(pr)])], axis=1)

    def av_first(pr, a):
        return _dot_nt(a[:, :KEY_BLOCK], pair_t(vl_ref.at[0], pr)) + _dot(a[:, KEY_BLOCK:], vd_s[:, _pair_lanes(pr)])

    _sb_pass(qk_first, av_first, 2 * KEY_BLOCK, _sb_diag_mask(tq), True, *scratch)

    def tail_pass(j):
        cols = _key_rows(j)
        pltpu.sync_copy(kc_hbm.at[b, :, :, cols], kt_s)
        pltpu.sync_copy(vc_hbm.at[b, :, :, cols], vt_s)
        _sb_pass(lambda pr, q: _dot(q, pair_t(kt_s, pr)), lambda pr, a: _dot_nt(a, pair_t(vt_s, pr)),
                 KEY_BLOCK, None, False, *scratch)

    _sb_sweep(n_past_blocks - 2, tail_pass, car_s)
    _sb_finish(o_ref, acc_s)


def _sb_scratch(tq):
    rows = D_HEADS * tq
    return [
        pltpu.VMEM((rows, LANES), BF16),
        pltpu.VMEM((rows, LANES), F32),
        pltpu.VMEM((rows, LANES), F32),
        pltpu.VMEM((4 * KEY_BLOCK, 2 * KEY_BLOCK), BF16),
        pltpu.VMEM((rows, 2 * KEY_BLOCK), F32),
        pltpu.VMEM((rows, 4 * KEY_BLOCK), BF16),
        pltpu.VMEM((rows, 2 * KEY_BLOCK), BF16),
    ]


def _sb_self(q, kb, vb):
    nb, t, _ = q.shape
    assert t % KEY_BLOCK == 0
    full = pl.BlockSpec((1, t, D_BRANCH), lambda b, i: (b, 0, 0))
    blk = pl.BlockSpec((1, KEY_BLOCK, D_BRANCH), lambda b, i: (b, i, 0))
    return pl.pallas_call(
        _sb_self_kernel,
        grid=(nb, t // KEY_BLOCK),
        in_specs=[blk, full, full],
        out_specs=blk,
        out_shape=jax.ShapeDtypeStruct((nb, t, D_BRANCH), F32),
        scratch_shapes=_sb_scratch(KEY_BLOCK),
        compiler_params=pltpu.CompilerParams(
            dimension_semantics=("arbitrary", "arbitrary"),
            vmem_limit_bytes=VMEM_LIMIT_BYTES),
        name="sb_self",
    )(q, kb, vb)


def _sb_cached(q, kb_new, vb_new, k_cache, v_cache):
    nb, tq, _ = q.shape
    t_past = k_cache.shape[-1]
    assert tq <= KEY_BLOCK and tq % (2 * SUBLANES) == 0 and t_past % KEY_BLOCK == 0
    n_past_blocks = t_past // KEY_BLOCK
    new = pl.BlockSpec((1, tq, D_BRANCH), lambda b: (b, 0, 0))
    last = pl.BlockSpec((1, D_HEADS, D_HEAD_DIM, KEY_BLOCK), lambda b: (b, 0, 0, n_past_blocks - 1))
    anywhere = pl.BlockSpec(memory_space=pl.ANY)
    cache_blk = pltpu.VMEM((D_HEADS, D_HEAD_DIM, KEY_BLOCK), F32)
    return pl.pallas_call(
        functools.partial(_sb_cached_kernel, n_past_blocks=n_past_blocks),
        grid=(nb,),
        in_specs=[new, new, new, last, last, anywhere, anywhere],
        out_specs=new,
        out_shape=jax.ShapeDtypeStruct((nb, tq, D_BRANCH), F32),
        scratch_shapes=_sb_scratch(tq) + [
            pltpu.VMEM((KEY_BLOCK, D_BRANCH), BF16),
            pltpu.VMEM((KEY_BLOCK, D_BRANCH), BF16),
            cache_blk, cache_blk,
        ],
        compiler_params=pltpu.CompilerParams(
            dimension_semantics=("arbitrary",),
            vmem_limit_bytes=VMEM_LIMIT_BYTES),
        name="sb_cached",
    )(q, kb_new, vb_new, k_cache, v_cache, k_cache, v_cache)


def _cd_out_kernel(h_ref, cout_ref, o_ref, sdz_ref, p_ref, wout_ref, gate_ref, pproj_ref, fg_ref,
                   y_ref, *, bb, tt):
    m = bb * tt
    x = h_ref[...].reshape(m, D_MODEL)
    d_out = (o_ref[...] * sdz_ref[...]).reshape(m, D_BRANCH).astype(BF16)
    c_out = cout_ref[...].reshape(m, D_BRANCH)
    h = x + _dot(c_out, wout_ref[0:D_BRANCH, :]) + _dot(d_out, wout_ref[D_BRANCH:, :])
    gate = jax.nn.sigmoid(_dot(h.astype(BF16), gate_ref[...]))
    emb = _dot(p_ref[...].reshape(m, PLE_DIM).astype(BF16), pproj_ref[...])
    y_ref[...] = _rms(h + gate * emb, fg_ref[...]).reshape(bb, tt, D_MODEL)


def _cd_out(h, cout, o, sdz, p_all, layer, w_out, gate, pproj, fg, *, bb, tt):
    nb, t, _ = h.shape
    grid = (nb // bb, t // tt)
    tok = lambda d: pl.BlockSpec((bb, tt, d), lambda b, s: (b, s, 0))
    return pl.pallas_call(
        functools.partial(_cd_out_kernel, bb=bb, tt=tt),
        grid=grid,
        in_specs=[
            tok(D_MODEL), tok(D_BRANCH), tok(D_BRANCH), tok(D_BRANCH),
            pl.BlockSpec((None, bb, tt, PLE_DIM), lambda b, s: (layer, b, s, 0)),
            _const_spec((2 * D_BRANCH, D_MODEL)),
            _const_spec((D_MODEL, D_MODEL)),
            _const_spec((PLE_DIM, D_MODEL)),
            _const_spec((1, D_MODEL)),
        ],
        out_specs=tok(D_MODEL),
        out_shape=jax.ShapeDtypeStruct((nb, t, D_MODEL), F32),
        compiler_params=pltpu.CompilerParams(
            dimension_semantics=("arbitrary", "arbitrary"),
            vmem_limit_bytes=VMEM_LIMIT_BYTES),
        name="cd_out",
    )(h, cout, o, sdz, p_all, w_out, gate, pproj, fg)


def _trunk(x, p_all, hist_a, hist_b, k_cache, v_cache, w, *, bb, tt, n_sub):
    t = x.shape[1]
    chunk = min(t, C_CHUNK)
    h1, new_a, new_b = _ab_layer(
        x, p_all, 0, hist_a, hist_b, w["norm_g"][0], w["ab_w_in"], w["a_conv_w"], w["b_conv_w"],
        w["b_ln_g"], w["b_ln_b"], w["ab_w_out"], w["ple_gate"][0], w["ple_proj"][0],
        bb=bb, tt=tt, n_sub=n_sub)
    tt1 = 2 * tt if bb == 1 else tt
    cout, q, k, v, kb, vb, sdz, *cvn = _cd_proj(
        h1, w["norm_g"][1], w["cd_w_in"], w["c_ln_g"], w["c_ln_b"],
        w["c_ws"][:, :chunk, :chunk], w["c_bt"][:chunk], bb=bb, tt=tt1, chunk=chunk,
        want_cvn=k_cache is not None)
    if k_cache is None:
        o = _sb_self(q, kb, vb)
    else:
        o = _sb_cached(q, kb, vb, k_cache, v_cache)
    y = _cd_out(h1, cout, o, sdz, p_all, 1, w["cd_w_out"], w["ple_gate"][1], w["ple_proj"][1],
                w["final_g"], bb=bb, tt=tt1)
    return y, new_a, new_b, (cvn[0] if cvn else None), k, v


def kernel(x_prompt, x_sample, state_a_conv, state_b_conv, cache_d_k, cache_d_v, p_prompt, p_sample,
           norm_g, ple_gate, ple_proj, ab_w_in, a_conv_w, b_conv_w, b_ln_g, b_ln_b, ab_w_out,
           cd_w_in, c_ln_g, c_ln_b, c_ws, c_b, cd_w_out, final_g):
    bp, tp, _ = x_prompt.shape
    bs, ts, _ = x_sample.shape
    w = {
        "norm_g": norm_g.reshape(2, 1, D_MODEL),
        "ple_gate": ple_gate.astype(BF16),
        "ple_proj": ple_proj.astype(BF16),
        "ab_w_in": ab_w_in[0].astype(BF16),
        "a_conv_w": a_conv_w[0],
        "b_conv_w": b_conv_w[0],
        "b_ln_g": b_ln_g, "b_ln_b": b_ln_b,
        "ab_w_out": ab_w_out[0].astype(BF16),
        "cd_w_in": cd_w_in[0].astype(BF16),
        "c_ln_g": c_ln_g, "c_ln_b": c_ln_b,
        "c_ws": c_ws[0],
        "c_bt": c_b[0].T,
        "cd_w_out": cd_w_out[0].astype(BF16),
        "final_g": final_g.reshape(1, D_MODEL),
    }
    zeros_a = jnp.zeros((bp, HIST_A, D_BRANCH), F32)
    zeros_b = jnp.zeros((bp, HIST_B, D_BRANCH), F32)
    yp, pa, pb, _, pk, pv = _trunk(x_prompt, p_prompt, zeros_a, zeros_b, None, None, w,
                                   bb=1, tt=512, n_sub=2)
    k_cache = jnp.transpose(cache_d_k[0], (0, 2, 3, 1))
    v_cache = jnp.transpose(cache_d_v[0], (0, 2, 3, 1))
    ys, sa, sb, scv, sk, sv = _trunk(x_sample, p_sample, state_a_conv[0], state_b_conv[0],
                                     k_cache, v_cache, w, bb=8, tt=ts, n_sub=1)
    heads = lambda a: a.reshape(1, a.shape[0], a.shape[1], D_HEADS, D_HEAD_DIM)
    return (yp, ys, pa[None], sa[None], pb[None], sb[None], scv[None],
            heads(pk), heads(pv), heads(sk), heads(sv))
```

```python
import functools

import jax
import jax.numpy as jnp
from jax import lax
from jax.experimental import pallas as pl
from jax.experimental.pallas import tpu as pltpu

D_MODEL = 1024
PLE_DIM = 256
D_BRANCH = 512
CONV_A = 3
CONV_B = 31
C_HEADS = 4
C_CHUNK = 128
C_HEAD_DIM = D_BRANCH // C_HEADS
D_HEADS = 8
D_HEAD_DIM = D_BRANCH // D_HEADS
EPS = 1e-6

LANES = 128
SUBLANES = 8
VMEM_LIMIT_BYTES = 56 * 1024 * 1024

HIST_A = CONV_A - 1
HIST_B = CONV_B - 1
PAD_A = SUBLANES
PAD_B = 4 * SUBLANES
CONV_ROWS = 32
KEY_BLOCK = 128
STICK_LOG_FLOOR = -88.0
MASKED_SCORE = -1.0e4

F32 = jnp.float32
BF16 = jnp.bfloat16


def _rms(x, g):
    return x * lax.rsqrt(jnp.mean(x * x, axis=-1, keepdims=True) + EPS) * g


def _layernorm(x, g, b):
    mu = jnp.mean(x, axis=-1, keepdims=True)
    xc = x - mu
    return xc * lax.rsqrt(jnp.mean(xc * xc, axis=-1, keepdims=True) + EPS) * g + b


def _silu(x):
    return x * jax.nn.sigmoid(x)


def _dot(a, b):
    return jnp.dot(a, b, preferred_element_type=F32)


def _ab_layer_kernel(x_ref, p_ref, ha_ref, hb_ref, g_ref, win_ref, aw_ref, bw_ref,
                     lng_ref, lnb_ref, wout_ref, gate_ref, pproj_ref,
                     h_ref, na_ref, nb_ref,
                     sh_s, wa_s, wb_s, *bufs, bb, tt, n_sub):
    s = pl.program_id(1)
    ts = tt // n_sub
    msub = bb * ts
    ua, ub, pr, ab, xb, hs = (bufs[i * n_sub:(i + 1) * n_sub] for i in range(6))
    hist_a = slice(PAD_A - HIST_A, PAD_A)
    hist_b = slice(PAD_B - HIST_B, PAD_B)
    tail_a = slice(PAD_A + ts - HIST_A, PAD_A + ts)
    tail_b = slice(PAD_B + ts - HIST_B, PAD_B + ts)

    @pl.when(s == 0)
    def _load_history():
        ua[0][:, hist_a, :] = ha_ref[...]
        ub[0][:, hist_b, :] = hb_ref[...]

    for j in range(CONV_A):
        wa_s[j] = jnp.broadcast_to(aw_ref[j:j + 1, :], (SUBLANES, D_BRANCH))
    for j in range(CONV_B):
        wb_s[j] = jnp.broadcast_to(bw_ref[j:j + 1, :], (SUBLANES, D_BRANCH))

    def rows_of(tile):
        return jnp.concatenate([tile] * (CONV_ROWS // SUBLANES), axis=0)

    def col(i):
        return slice(i * D_BRANCH, (i + 1) * D_BRANCH)

    def project_jobs(k):
        t0 = k * ts

        def normalise():
            x = x_ref[:, t0:t0 + ts, :].reshape(msub, D_MODEL)
            xb[k][...] = _rms(x, g_ref[...]).astype(BF16)

        def proj(i):
            pr[k][:, col(i)] = _dot(xb[k][...], win_ref[:, col(i)])

        return [normalise] + [functools.partial(proj, i) for i in range(7)]

    def conv_jobs(k):
        def conv_inputs():
            ua[k][:, PAD_A:, :] = (pr[k][:, col(1)] * pr[k][:, col(0)]).reshape(bb, ts, D_BRANCH)
            ub[k][:, PAD_B:, :] = (
                pr[k][:, col(4)] * jax.nn.sigmoid(pr[k][:, col(5)])).reshape(bb, ts, D_BRANCH)
            if k > 0:
                ua[k][:, hist_a, :] = ua[k - 1][:, tail_a, :]
                ub[k][:, hist_b, :] = ub[k - 1][:, tail_b, :]

        def shifted_copies(b):
            for shift in range(1, SUBLANES):
                sh_s[shift - 1] = ub[k][b, shift:shift + PAD_B + ts - SUBLANES, :]

        def chunk(b, r0):
            if b == 0 and r0 == 0:
                conv_inputs()
            if r0 == 0:
                shifted_copies(b)
            rows = pl.ds(b * ts + r0, CONV_ROWS)
            conv_a = None
            for j in range(CONV_A):
                term = rows_of(wa_s[j]) * ua[k][b, pl.ds(PAD_A - HIST_A + r0 + j, CONV_ROWS), :]
                conv_a = term if conv_a is None else conv_a + term
            ab[k][rows, 0:D_BRANCH] = (pr[k][rows, col(2)] * _silu(pr[k][rows, col(3)]) * conv_a).astype(BF16)

            conv_b = None
            for j in range(CONV_B):
                blk, shift = divmod(PAD_B - HIST_B + j, SUBLANES)
                src = pl.ds(r0 + blk * SUBLANES, CONV_ROWS)
                tap = ub[k][b, src, :] if shift == 0 else sh_s[shift - 1, src, :]
                term = rows_of(wb_s[j]) * tap
                conv_b = term if conv_b is None else conv_b + term
            b_act = _silu(_layernorm(conv_b, lng_ref[...], lnb_ref[...]))
            ab[k][rows, D_BRANCH:2 * D_BRANCH] = (b_act * _silu(pr[k][rows, col(6)])).astype(BF16)

        return [functools.partial(chunk, b, r0) for b in range(bb) for r0 in range(0, ts, CONV_ROWS)]

    def finish_jobs(k):
        t0 = k * ts

        def residual():
            x = x_ref[:, t0:t0 + ts, :].reshape(msub, D_MODEL)
            hs[k][...] = x + _dot(ab[k][...], wout_ref[...])

        def embed():
            h = hs[k][...]
            gate = jax.nn.sigmoid(_dot(h.astype(BF16), gate_ref[...]))
            emb = _dot(p_ref[:, t0:t0 + ts, :].reshape(msub, PLE_DIM).astype(BF16), pproj_ref[...])
            h_ref[:, t0:t0 + ts, :] = (h + gate * emb).reshape(bb, ts, D_MODEL)

        return [residual, embed]

    def interleave(vpu_jobs, mxu_jobs):
        nv, nm = len(vpu_jobs), len(mxu_jobs)
        done = 0
        for i, job in enumerate(vpu_jobs):
            while done < nm and done * nv <= i * nm:
                mxu_jobs[done]()
                done += 1
            job()
        for job in mxu_jobs[done:]:
            job()

    interleave([], project_jobs(0))
    for k in range(n_sub):
        mxu_jobs = (finish_jobs(k - 1) if k > 0 else []) + (project_jobs(k + 1) if k + 1 < n_sub else [])
        interleave(conv_jobs(k), mxu_jobs)
    interleave([], finish_jobs(n_sub - 1))

    last_a = ua[n_sub - 1][:, tail_a, :]
    last_b = ub[n_sub - 1][:, tail_b, :]
    ua[0][:, hist_a, :] = last_a
    ub[0][:, hist_b, :] = last_b

    @pl.when(s == pl.num_programs(1) - 1)
    def _store_state():
        na_ref[...] = last_a
        nb_ref[...] = last_b


def _const_spec(shape):
    return pl.BlockSpec(shape, lambda b, s: (0,) * len(shape))


def _ab_layer(x, p_all, layer, hist_a, hist_b, g, w_in, a_w, b_w, ln_g, ln_b, w_out, gate, pproj,
              *, bb, tt, n_sub):
    nb, t, _ = x.shape
    assert nb % bb == 0 and t % tt == 0 and tt % n_sub == 0 and (bb == 1 or n_sub == 1)
    ts = tt // n_sub
    assert ts % CONV_ROWS == 0 and ts >= HIST_B
    msub = bb * ts
    grid = (nb // bb, t // tt)
    tok = lambda d: pl.BlockSpec((bb, tt, d), lambda b, s: (b, s, 0))
    state = lambda r: pl.BlockSpec((bb, r, D_BRANCH), lambda b, s: (b, 0, 0))
    return pl.pallas_call(
        functools.partial(_ab_layer_kernel, bb=bb, tt=tt, n_sub=n_sub),
        grid=grid,
        in_specs=[
            tok(D_MODEL),
            pl.BlockSpec((None, bb, tt, PLE_DIM), lambda b, s: (layer, b, s, 0)),
            state(HIST_A), state(HIST_B),
            _const_spec((1, D_MODEL)),
            _const_spec((D_MODEL, 7 * D_BRANCH)),
            _const_spec((CONV_A, D_BRANCH)),
            _const_spec((CONV_B, D_BRANCH)),
            _const_spec((1, D_BRANCH)), _const_spec((1, D_BRANCH)),
            _const_spec((2 * D_BRANCH, D_MODEL)),
            _const_spec((D_MODEL, D_MODEL)),
            _const_spec((PLE_DIM, D_MODEL)),
        ],
        out_specs=[tok(D_MODEL), state(HIST_A), state(HIST_B)],
        out_shape=[
            jax.ShapeDtypeStruct((nb, t, D_MODEL), F32),
            jax.ShapeDtypeStruct((nb, HIST_A, D_BRANCH), F32),
            jax.ShapeDtypeStruct((nb, HIST_B, D_BRANCH), F32),
        ],
        scratch_shapes=[
            pltpu.VMEM((SUBLANES - 1, PAD_B + ts - SUBLANES, D_BRANCH), F32),
            pltpu.VMEM((CONV_A, SUBLANES, D_BRANCH), F32),
            pltpu.VMEM((CONV_B, SUBLANES, D_BRANCH), F32),
        ] + [
            pltpu.VMEM(shape, dtype)
            for shape, dtype in [
                ((bb, PAD_A + ts, D_BRANCH), F32),
                ((bb, PAD_B + ts, D_BRANCH), F32),
                ((msub, 7 * D_BRANCH), F32),
                ((msub, 2 * D_BRANCH), BF16),
                ((msub, D_MODEL), BF16),
                ((msub, D_MODEL), F32),
            ]
            for _ in range(n_sub)
        ],
        compiler_params=pltpu.CompilerParams(
            dimension_semantics=("arbitrary", "arbitrary"),
            vmem_limit_bytes=VMEM_LIMIT_BYTES),
        name="ab_layer",
    )(x, p_all, hist_a, hist_b, g, w_in, a_w, b_w, ln_g, ln_b, w_out, gate, pproj)


def _cd_proj_kernel(h_ref, g_ref, win_ref, lng_ref, lnb_ref, ws_ref, cbt_ref,
                    cout_ref, q_ref, k_ref, v_ref, kb_ref, vb_ref, sdz_ref, *rest, bb, tt, chunk):
    cvn_s, gc_s = rest[-2:]
    m = bb * tt
    x = h_ref[...].reshape(m, D_MODEL)
    xb = _rms(x, g_ref[...]).astype(BF16)

    def proj(i):
        return _dot(xb, win_ref[:, i * D_BRANCH:(i + 1) * D_BRANCH])

    gc_s[...] = proj(0) * _silu(proj(2))
    cvn = _layernorm(proj(1), lng_ref[...], lnb_ref[...])
    if len(rest) == 3:
        rest[0][...] = cvn.reshape(bb, tt, D_BRANCH)
    cvn_s[...] = cvn.astype(BF16)
    q_ref[...] = (proj(3) * (D_HEAD_DIM ** -0.5)).astype(BF16).reshape(bb, tt, D_BRANCH)
    k = proj(4).reshape(bb, tt, D_BRANCH)
    k_ref[...] = k
    kb_ref[...] = k.astype(BF16)
    v = proj(5).reshape(bb, tt, D_BRANCH)
    v_ref[...] = v
    vb_ref[...] = v.astype(BF16)
    sdz_ref[...] = _silu(proj(6)).reshape(bb, tt, D_BRANCH)

    row = lax.broadcasted_iota(jnp.int32, (chunk, chunk), 0)
    col = lax.broadcasted_iota(jnp.int32, (chunk, chunk), 1)
    for hd in range(C_HEADS):
        lanes = slice(hd * C_HEAD_DIM, (hd + 1) * C_HEAD_DIM)
        w_tril = jnp.where(col <= row, ws_ref[hd], 0.0).astype(BF16)
        bias = cbt_ref[:, hd:hd + 1]
        for c0 in range(0, m, chunk):
            rows = pl.ds(c0, chunk)
            mixed = _dot(w_tril, cvn_s[rows, lanes]) + bias
            cout_ref[c0 // tt, pl.ds(c0 % tt, chunk), lanes] = (gc_s[rows, lanes] * mixed).astype(BF16)


def _cd_proj(h, g, w_in, ln_g, ln_b, ws, cbt, *, bb, tt, chunk, want_cvn):
    nb, t, _ = h.shape
    assert nb % bb == 0 and t % tt == 0 and tt % chunk == 0
    m = bb * tt
    grid = (nb // bb, t // tt)
    tok = lambda d: pl.BlockSpec((bb, tt, d), lambda b, s: (b, s, 0))
    shp = lambda dt: jax.ShapeDtypeStruct((nb, t, D_BRANCH), dt)
    return pl.pallas_call(
        functools.partial(_cd_proj_kernel, bb=bb, tt=tt, chunk=chunk),
        grid=grid,
        in_specs=[
            tok(D_MODEL),
            _const_spec((1, D_MODEL)),
            _const_spec((D_MODEL, 7 * D_BRANCH)),
            _const_spec((1, D_BRANCH)), _const_spec((1, D_BRANCH)),
            _const_spec((C_HEADS, chunk, chunk)),
            _const_spec((chunk, C_HEADS)),
        ],
        out_specs=[tok(D_BRANCH)] * (8 if want_cvn else 7),
        out_shape=[shp(BF16), shp(BF16), shp(F32), shp(F32), shp(BF16), shp(BF16), shp(F32)]
        + ([shp(F32)] if want_cvn else []),
        scratch_shapes=[
            pltpu.VMEM((m, D_BRANCH), BF16),
            pltpu.VMEM((m, D_BRANCH), F32),
        ],
        compiler_params=pltpu.CompilerParams(
            dimension_semantics=("arbitrary", "arbitrary"),
            vmem_limit_bytes=VMEM_LIMIT_BYTES),
        name="cd_proj",
    )(h, g, w_in, ln_g, ln_b, ws, cbt)


def _dot_nt(a, b):
    return lax.dot_general(a, b, (((1,), (1,)), ((), ())), preferred_element_type=F32)


def _pair_lanes(pr):
    return slice(pr * LANES, (pr + 1) * LANES)


def _sb_setup(q, qm_s, tri_s):
    tq = q.shape[0]
    lane = lax.broadcasted_iota(jnp.int32, (tq, LANES), 1)
    for hd in range(D_HEADS):
        pair = q[:, _pair_lanes(hd // 2)]
        in_head = (lane >= D_HEAD_DIM) if hd % 2 else (lane < D_HEAD_DIM)
        qm_s[hd * tq:(hd + 1) * tq, :] = jnp.where(in_head, pair, jnp.zeros_like(pair))
    n = tri_s.shape[1]
    row = lax.broadcasted_iota(jnp.int32, (n, n), 0)
    col = lax.broadcasted_iota(jnp.int32, (n, n), 1)
    tri = jnp.where(row > col, 1.0, 0.0).astype(BF16)
    tri_s[0:n, :] = tri
    tri_s[n:2 * n, :] = tri


def _sb_diag_mask(tq):
    shape = (2 * tq, KEY_BLOCK)
    r = lax.broadcasted_iota(jnp.int32, shape, 0)
    qi = jnp.where(r >= tq, r - tq, r)
    return lax.broadcasted_iota(jnp.int32, shape, 1) < qi


def _sb_pass(qk, av, nk, diag_valid, first, qm_s, acc_s, car_s, tri_s, lb_s, hl_s, a_s):
    m2 = qm_s.shape[0] // (D_HEADS // 2)
    pairs = [pl.ds(pr * m2, m2) for pr in range(D_HEADS // 2)]
    tri2 = jnp.concatenate([tri_s[0:nk, 0:nk], tri_s[2 * KEY_BLOCK:2 * KEY_BLOCK + nk, 0:nk]], axis=0)

    def masked(z):
        if diag_valid is None:
            return z
        return jnp.concatenate(
            [z[:, :nk - KEY_BLOCK], jnp.where(diag_valid, z[:, nk - KEY_BLOCK:], MASKED_SCORE)], axis=1)

    for pr, rows in enumerate(pairs):
        z = masked(qk(pr, qm_s[rows, :]))
        soft = jnp.log(1.0 + jnp.exp(-jnp.abs(z)))
        log_beta = jnp.minimum(z, 0.0) - soft
        log_keep = log_beta - z
        hi = log_keep.astype(BF16)
        hl_s[rows, 0:nk] = hi
        hl_s[rows, nk:2 * nk] = (log_keep - hi.astype(F32)).astype(BF16)
        total = jnp.sum(log_keep, axis=-1, keepdims=True)
        if first:
            lb_s[rows, 0:nk] = log_beta
            car_s[rows, :] = jnp.broadcast_to(total, (m2, LANES))
        else:
            carry = car_s[rows, :]
            lb_s[rows, 0:nk] = log_beta + jnp.concatenate([carry] * (nk // LANES), axis=1)
            car_s[rows, :] = carry + total
    for pr, rows in enumerate(pairs):
        later = _dot(hl_s[rows, 0:2 * nk], tri2)
        a_s[rows, 0:nk] = jnp.exp(lb_s[rows, 0:nk] + later).astype(BF16)
    for pr, rows in enumerate(pairs):
        pv = av(pr, a_s[rows, 0:nk])
        acc_s[rows, :] = pv if first else acc_s[rows, :] + pv


def _sb_all_spent(car_s):
    return (jnp.max(car_s[...]) < STICK_LOG_FLOOR).astype(jnp.int32)


def _sb_sweep(first_block, tail_pass, car_s):
    def cond(c):
        j, spent = c
        return jnp.logical_and(j >= 0, spent == 0)

    def body(c):
        j, _ = c
        tail_pass(j)
        return j - 1, _sb_all_spent(car_s)

    lax.while_loop(cond, body, (jnp.asarray(first_block, jnp.int32), _sb_all_spent(car_s)))


def _sb_finish(o_ref, gate_ref, acc_s):
    tq = acc_s.shape[0] // D_HEADS
    lane = lax.broadcasted_iota(jnp.int32, (tq, LANES), 1)
    for pr in range(D_HEADS // 2):
        even = acc_s[(2 * pr) * tq:(2 * pr + 1) * tq, :]
        odd = acc_s[(2 * pr + 1) * tq:(2 * pr + 2) * tq, :]
        o = jnp.where(lane < D_HEAD_DIM, even, odd)
        o_ref[0, :, _pair_lanes(pr)] = (o * gate_ref[0, :, _pair_lanes(pr)]).astype(o_ref.dtype)


def _key_rows(j):
    return pl.ds(pl.multiple_of(j * KEY_BLOCK, KEY_BLOCK), KEY_BLOCK)


def _sb_self_kernel(q_ref, k_ref, v_ref, gate_ref, o_ref, qm_s, acc_s, car_s, tri_s, lb_s, hl_s, a_s):
    i = pl.program_id(1)
    tq = q_ref.shape[1]
    scratch = (qm_s, acc_s, car_s, tri_s, lb_s, hl_s, a_s)
    _sb_setup(q_ref[0], qm_s, tri_s)
    prev, diag = _key_rows(jnp.maximum(i - 1, 0)), _key_rows(i)

    def keys(pr):
        return jnp.concatenate([k_ref[0, prev, _pair_lanes(pr)], k_ref[0, diag, _pair_lanes(pr)]], axis=0)

    def values(pr):
        v_prev = v_ref[0, prev, _pair_lanes(pr)]
        v_prev = jnp.where(jnp.broadcast_to(i, v_prev.shape) > 0, v_prev, jnp.zeros_like(v_prev))
        return jnp.concatenate([v_prev, v_ref[0, diag, _pair_lanes(pr)]], axis=0)

    _sb_pass(lambda pr, q: _dot_nt(q, keys(pr)), lambda pr, a: _dot(a, values(pr)),
             2 * KEY_BLOCK, _sb_diag_mask(tq), True, *scratch)

    def tail_pass(j):
        rows = _key_rows(j)
        _sb_pass(lambda pr, q: _dot_nt(q, k_ref[0, rows, _pair_lanes(pr)]),
                 lambda pr, a: _dot(a, v_ref[0, rows, _pair_lanes(pr)]),
                 KEY_BLOCK, None, False, *scratch)

    _sb_sweep(i - 2, tail_pass, car_s)
    _sb_finish(o_ref, gate_ref, acc_s)


def _sb_cached_kernel(q_ref, kn_ref, vn_ref, gate_ref, kl_ref, vl_ref, kc_hbm, vc_hbm, o_ref,
                      qm_s, acc_s, car_s, tri_s, lb_s, hl_s, a_s, kd_s, vd_s, kt_s, vt_s,
                      *, n_past_blocks):
    b = pl.program_id(0)
    tq = q_ref.shape[1]
    tn = kn_ref.shape[1]
    scratch = (qm_s, acc_s, car_s, tri_s, lb_s, hl_s, a_s)
    _sb_setup(q_ref[0], qm_s, tri_s)
    kd_s[...] = jnp.zeros_like(kd_s)
    vd_s[...] = jnp.zeros_like(vd_s)
    kd_s[0:tn, :] = kn_ref[0]
    vd_s[0:tn, :] = vn_ref[0]

    def pair_t(ref, pr):
        return ref[2 * pr:2 * pr + 2].reshape(LANES, KEY_BLOCK).astype(BF16)

    def qk_first(pr, q):
        return jnp.concatenate([_dot(q, pair_t(kl_ref.at[0], pr)), _dot_nt(q, kd_s[:, _pair_lanes(pr)])], axis=1)

    def av_first(pr, a):
        return _dot_nt(a[:, :KEY_BLOCK], pair_t(vl_ref.at[0], pr)) + _dot(a[:, KEY_BLOCK:], vd_s[:, _pair_lanes(pr)])

    _sb_pass(qk_first, av_first, 2 * KEY_BLOCK, _sb_diag_mask(tq), True, *scratch)

    def tail_pass(j):
        cols = _key_rows(j)
        pltpu.sync_copy(kc_hbm.at[b, :, :, cols], kt_s)
        pltpu.sync_copy(vc_hbm.at[b, :, :, cols], vt_s)
        _sb_pass(lambda pr, q: _dot(q, pair_t(kt_s, pr)), lambda pr, a: _dot_nt(a, pair_t(vt_s, pr)),
                 KEY_BLOCK, None, False, *scratch)

    _sb_sweep(n_past_blocks - 2, tail_pass, car_s)
    _sb_finish(o_ref, gate_ref, acc_s)


def _sb_scratch(tq):
    rows = D_HEADS * tq
    return [
        pltpu.VMEM((rows, LANES), BF16),
        pltpu.VMEM((rows, LANES), F32),
        pltpu.VMEM((rows, LANES), F32),
        pltpu.VMEM((4 * KEY_BLOCK, 2 * KEY_BLOCK), BF16),
        pltpu.VMEM((rows, 2 * KEY_BLOCK), F32),
        pltpu.VMEM((rows, 4 * KEY_BLOCK), BF16),
        pltpu.VMEM((rows, 2 * KEY_BLOCK), BF16),
    ]


def _sb_self(q, kb, vb, gate):
    nb, t, _ = q.shape
    assert t % KEY_BLOCK == 0
    full = pl.BlockSpec((1, t, D_BRANCH), lambda b, i: (b, 0, 0))
    blk = pl.BlockSpec((1, KEY_BLOCK, D_BRANCH), lambda b, i: (b, i, 0))
    return pl.pallas_call(
        _sb_self_kernel,
        grid=(nb, t // KEY_BLOCK),
        in_specs=[blk, full, full, blk],
        out_specs=blk,
        out_shape=jax.ShapeDtypeStruct((nb, t, D_BRANCH), BF16),
        scratch_shapes=_sb_scratch(KEY_BLOCK),
        compiler_params=pltpu.CompilerParams(
            dimension_semantics=("arbitrary", "arbitrary"),
            vmem_limit_bytes=VMEM_LIMIT_BYTES),
        name="sb_self",
    )(q, kb, vb, gate)


def _sb_cached(q, kb_new, vb_new, gate, k_cache, v_cache):
    nb, tq, _ = q.shape
    t_past = k_cache.shape[-1]
    assert tq <= KEY_BLOCK and tq % (2 * SUBLANES) == 0 and t_past % KEY_BLOCK == 0
    n_past_blocks = t_past // KEY_BLOCK
    new = pl.BlockSpec((1, tq, D_BRANCH), lambda b: (b, 0, 0))
    last = pl.BlockSpec((1, D_HEADS, D_HEAD_DIM, KEY_BLOCK), lambda b: (b, 0, 0, n_past_blocks - 1))
    anywhere = pl.BlockSpec(memory_space=pl.ANY)
    cache_blk = pltpu.VMEM((D_HEADS, D_HEAD_DIM, KEY_BLOCK), F32)
    return pl.pallas_call(
        functools.partial(_sb_cached_kernel, n_past_blocks=n_past_blocks),
        grid=(nb,),
        in_specs=[new, new, new, new, last, last, anywhere, anywhere],
        out_specs=new,
        out_shape=jax.ShapeDtypeStruct((nb, tq, D_BRANCH), BF16),
        scratch_shapes=_sb_scratch(tq) + [
            pltpu.VMEM((KEY_BLOCK, D_BRANCH), BF16),
            pltpu.VMEM((KEY_BLOCK, D_BRANCH), BF16),
            cache_blk, cache_blk,
        ],
        compiler_params=pltpu.CompilerParams(
            dimension_semantics=("arbitrary",),
            vmem_limit_bytes=VMEM_LIMIT_BYTES),
        name="sb_cached",
    )(q, kb_new, vb_new, gate, k_cache, v_cache, k_cache, v_cache)


def _cd_out_kernel(h_ref, cout_ref, dout_ref, p_ref, wout_ref, gate_ref, pproj_ref, fg_ref,
                   y_ref, *, bb, tt):
    m = bb * tt
    x = h_ref[...].reshape(m, D_MODEL)
    d_out = dout_ref[...].reshape(m, D_BRANCH)
    c_out = cout_ref[...].reshape(m, D_BRANCH)
    h = x + _dot(c_out, wout_ref[0:D_BRANCH, :]) + _dot(d_out, wout_ref[D_BRANCH:, :])
    gate = jax.nn.sigmoid(_dot(h.astype(BF16), gate_ref[...]))
    emb = _dot(p_ref[...].reshape(m, PLE_DIM).astype(BF16), pproj_ref[...])
    y_ref[...] = _rms(h + gate * emb, fg_ref[...]).reshape(bb, tt, D_MODEL)


def _cd_out(h, cout, dout, p_all, layer, w_out, gate, pproj, fg, *, bb, tt):
    nb, t, _ = h.shape
    grid = (nb // bb, t // tt)
    tok = lambda d: pl.BlockSpec((bb, tt, d), lambda b, s: (b, s, 0))
    return pl.pallas_call(
        functools.partial(_cd_out_kernel, bb=bb, tt=tt),
        grid=grid,
        in_specs=[
            tok(D_MODEL), tok(D_BRANCH), tok(D_BRANCH),
            pl.BlockSpec((None, bb, tt, PLE_DIM), lambda b, s: (layer, b, s, 0)),
            _const_spec((2 * D_BRANCH, D_MODEL)),
            _const_spec((D_MODEL, D_MODEL)),
            _const_spec((PLE_DIM, D_MODEL)),
            _const_spec((1, D_MODEL)),
        ],
        out_specs=tok(D_MODEL),
        out_shape=jax.ShapeDtypeStruct((nb, t, D_MODEL), F32),
        compiler_params=pltpu.CompilerParams(
            dimension_semantics=("arbitrary", "arbitrary"),
            vmem_limit_bytes=VMEM_LIMIT_BYTES),
        name="cd_out",
    )(h, cout, dout, p_all, w_out, gate, pproj, fg)


def _trunk(x, p_all, hist_a, hist_b, k_cache, v_cache, w, *, bb, tt, n_sub):
    t = x.shape[1]
    chunk = min(t, C_CHUNK)
    h1, new_a, new_b = _ab_layer(
        x, p_all, 0, hist_a, hist_b, w["norm_g"][0], w["ab_w_in"], w["a_conv_w"], w["b_conv_w"],
        w["b_ln_g"], w["b_ln_b"], w["ab_w_out"], w["ple_gate"][0], w["ple_proj"][0],
        bb=bb, tt=tt, n_sub=n_sub)
    tt1 = 2 * tt if bb == 1 else tt
    cout, q, k, v, kb, vb, sdz, *cvn = _cd_proj(
        h1, w["norm_g"][1], w["cd_w_in"], w["c_ln_g"], w["c_ln_b"],
        w["c_ws"][:, :chunk, :chunk], w["c_bt"][:chunk], bb=bb, tt=tt1, chunk=chunk,
        want_cvn=k_cache is not None)
    if k_cache is None:
        dout = _sb_self(q, kb, vb, sdz)
    else:
        dout = _sb_cached(q, kb, vb, sdz, k_cache, v_cache)
    y = _cd_out(h1, cout, dout, p_all, 1, w["cd_w_out"], w["ple_gate"][1], w["ple_proj"][1],
                w["final_g"], bb=bb, tt=tt1)
    return y, new_a, new_b, (cvn[0] if cvn else None), k, v


def kernel(x_prompt, x_sample, state_a_conv, state_b_conv, cache_d_k, cache_d_v, p_prompt, p_sample,
           norm_g, ple_gate, ple_proj, ab_w_in, a_conv_w, b_conv_w, b_ln_g, b_ln_b, ab_w_out,
           cd_w_in, c_ln_g, c_ln_b, c_ws, c_b, cd_w_out, final_g):
    bp, tp, _ = x_prompt.shape
    bs, ts, _ = x_sample.shape
    w = {
        "norm_g": norm_g.reshape(2, 1, D_MODEL),
        "ple_gate": ple_gate.astype(BF16),
        "ple_proj": ple_proj.astype(BF16),
        "ab_w_in": ab_w_in[0].astype(BF16),
        "a_conv_w": a_conv_w[0],
        "b_conv_w": b_conv_w[0],
        "b_ln_g": b_ln_g, "b_ln_b": b_ln_b,
        "ab_w_out": ab_w_out[0].astype(BF16),
        "cd_w_in": cd_w_in[0].astype(BF16),
        "c_ln_g": c_ln_g, "c_ln_b": c_ln_b,
        "c_ws": c_ws[0],
        "c_bt": c_b[0].T,
        "cd_w_out": cd_w_out[0].astype(BF16),
        "final_g": final_g.reshape(1, D_MODEL),
    }
    zeros_a = jnp.zeros((bp, HIST_A, D_BRANCH), F32)
    zeros_b = jnp.zeros((bp, HIST_B, D_BRANCH), F32)
    yp, pa, pb, _, pk, pv = _trunk(x_prompt, p_prompt, zeros_a, zeros_b, None, None, w,
                                   bb=1, tt=512, n_sub=2)
    k_cache = jnp.transpose(cache_d_k[0], (0, 2, 3, 1))
    v_cache = jnp.transpose(cache_d_v[0], (0, 2, 3, 1))
    ys, sa, sb, scv, sk, sv = _trunk(x_sample, p_sample, state_a_conv[0], state_b_conv[0],
                                     k_cache, v_cache, w, bb=8, tt=ts, n_sub=1)
    heads = lambda a: a.reshape(1, a.shape[0], a.shape[1], D_HEADS, D_HEAD_DIM)
    return (yp, ys, pa[None], sa[None], pb[None], sb[None], scv[None],
            heads(pk), heads(pv), heads(sk), heads(sv))
```

```python
import functools

import jax
import jax.numpy as jnp
from jax import lax
from jax.experimental import pallas as pl
from jax.experimental.pallas import tpu as pltpu

D_MODEL = 1024
PLE_DIM = 256
D_BRANCH = 512
CONV_A = 3
CONV_B = 31
C_HEADS = 4
C_CHUNK = 128
C_HEAD_DIM = D_BRANCH // C_HEADS
D_HEADS = 8
D_HEAD_DIM = D_BRANCH // D_HEADS
EPS = 1e-6

LANES = 128
SUBLANES = 8
VMEM_LIMIT_BYTES = 56 * 1024 * 1024

HIST_A = CONV_A - 1
HIST_B = CONV_B - 1
PAD_A = SUBLANES
PAD_B = 4 * SUBLANES
CONV_ROWS = 32
KEY_BLOCK = 128
STICK_LOG_FLOOR = -88.0
MASKED_SCORE = -1.0e4

F32 = jnp.float32
BF16 = jnp.bfloat16


def _rms(x, g):
    return x * lax.rsqrt(jnp.mean(x * x, axis=-1, keepdims=True) + EPS) * g


def _layernorm(x, g, b):
    mu = jnp.mean(x, axis=-1, keepdims=True)
    xc = x - mu
    return xc * lax.rsqrt(jnp.mean(xc * xc, axis=-1, keepdims=True) + EPS) * g + b


def _silu(x):
    return x * jax.nn.sigmoid(x)


def _dot(a, b):
    return jnp.dot(a, b, preferred_element_type=F32)


def _ab_layer_kernel(x_ref, p_ref, ha_ref, hb_ref, g_ref, win_ref, aw_ref, bw_ref,
                     lng_ref, lnb_ref, wout_ref, gate_ref, pproj_ref,
                     h_ref, na_ref, nb_ref,
                     sh_s, wa_s, wb_s, *bufs, bb, tt, n_sub):
    s = pl.program_id(1)
    ts = tt // n_sub
    msub = bb * ts
    ua, ub, pr, ab, xb, hs = (bufs[i * n_sub:(i + 1) * n_sub] for i in range(6))
    hist_a = slice(PAD_A - HIST_A, PAD_A)
    hist_b = slice(PAD_B - HIST_B, PAD_B)
    tail_a = slice(PAD_A + ts - HIST_A, PAD_A + ts)
    tail_b = slice(PAD_B + ts - HIST_B, PAD_B + ts)

    @pl.when(s == 0)
    def _load_history():
        ua[0][:, hist_a, :] = ha_ref[...]
        ub[0][:, hist_b, :] = hb_ref[...]

    for j in range(CONV_A):
        wa_s[j] = jnp.broadcast_to(aw_ref[j:j + 1, :], (SUBLANES, D_BRANCH))
    for j in range(CONV_B):
        wb_s[j] = jnp.broadcast_to(bw_ref[j:j + 1, :], (SUBLANES, D_BRANCH))

    def rows_of(tile):
        return jnp.concatenate([tile] * (CONV_ROWS // SUBLANES), axis=0)

    def col(i):
        return slice(i * D_BRANCH, (i + 1) * D_BRANCH)

    def project_jobs(k):
        t0 = k * ts

        def normalise():
            x = x_ref[:, t0:t0 + ts, :].reshape(msub, D_MODEL)
            xb[k][...] = _rms(x, g_ref[...]).astype(BF16)

        def proj(i):
            pr[k][:, col(i)] = _dot(xb[k][...], win_ref[:, col(i)])

        return [normalise] + [functools.partial(proj, i) for i in range(7)]

    def conv_jobs(k):
        def conv_inputs():
            ua[k][:, PAD_A:, :] = (pr[k][:, col(1)] * pr[k][:, col(0)]).reshape(bb, ts, D_BRANCH)
            ub[k][:, PAD_B:, :] = (
                pr[k][:, col(4)] * jax.nn.sigmoid(pr[k][:, col(5)])).reshape(bb, ts, D_BRANCH)
            if k > 0:
                ua[k][:, hist_a, :] = ua[k - 1][:, tail_a, :]
                ub[k][:, hist_b, :] = ub[k - 1][:, tail_b, :]

        def shifted_copies(b):
            for shift in range(1, SUBLANES):
                sh_s[shift - 1] = ub[k][b, shift:shift + PAD_B + ts - SUBLANES, :]

        def chunk(b, r0):
            if b == 0 and r0 == 0:
                conv_inputs()
            if r0 == 0:
                shifted_copies(b)
            rows = pl.ds(b * ts + r0, CONV_ROWS)
            conv_a = None
            for j in range(CONV_A):
                term = rows_of(wa_s[j]) * ua[k][b, pl.ds(PAD_A - HIST_A + r0 + j, CONV_ROWS), :]
                conv_a = term if conv_a is None else conv_a + term
            ab[k][rows, 0:D_BRANCH] = (pr[k][rows, col(2)] * _silu(pr[k][rows, col(3)]) * conv_a).astype(BF16)

            conv_b = None
            for j in range(CONV_B):
                blk, shift = divmod(PAD_B - HIST_B + j, SUBLANES)
                src = pl.ds(r0 + blk * SUBLANES, CONV_ROWS)
                tap = ub[k][b, src, :] if shift == 0 else sh_s[shift - 1, src, :]
                term = rows_of(wb_s[j]) * tap
                conv_b = term if conv_b is None else conv_b + term
            b_act = _silu(_layernorm(conv_b, lng_ref[...], lnb_ref[...]))
            ab[k][rows, D_BRANCH:2 * D_BRANCH] = (b_act * _silu(pr[k][rows, col(6)])).astype(BF16)

        return [functools.partial(chunk, b, r0) for b in range(bb) for r0 in range(0, ts, CONV_ROWS)]

    def finish_jobs(k):
        t0 = k * ts

        def residual():
            x = x_ref[:, t0:t0 + ts, :].reshape(msub, D_MODEL)
            hs[k][...] = x + _dot(ab[k][...], wout_ref[...])

        def embed():
            h = hs[k][...]
            gate = jax.nn.sigmoid(_dot(h.astype(BF16), gate_ref[...]))
            emb = _dot(p_ref[:, t0:t0 + ts, :].reshape(msub, PLE_DIM).astype(BF16), pproj_ref[...])
            h_ref[:, t0:t0 + ts, :] = (h + gate * emb).reshape(bb, ts, D_MODEL)

        return [residual, embed]

    def interleave(vpu_jobs, mxu_jobs):
        nv, nm = len(vpu_jobs), len(mxu_jobs)
        done = 0
        for i, job in enumerate(vpu_jobs):
            while done < nm and done * nv <= i * nm:
                mxu_jobs[done]()
                done += 1
            job()
        for job in mxu_jobs[done:]:
            job()

    interleave([], project_jobs(0))
    for k in range(n_sub):
        mxu_jobs = (finish_jobs(k - 1) if k > 0 else []) + (project_jobs(k + 1) if k + 1 < n_sub else [])
        interleave(conv_jobs(k), mxu_jobs)
    interleave([], finish_jobs(n_sub - 1))

    last_a = ua[n_sub - 1][:, tail_a, :]
    last_b = ub[n_sub - 1][:, tail_b, :]
    ua[0][:, hist_a, :] = last_a
    ub[0][:, hist_b, :] = last_b

    @pl.when(s == pl.num_programs(1) - 1)
    def _store_state():
        na_ref[...] = last_a
        nb_ref[...] = last_b


def _const_spec(shape):
    return pl.BlockSpec(shape, lambda b, s: (0,) * len(shape), pipeline_mode=pl.Buffered(1))


def _stream_spec(shape, index_map):
    return pl.BlockSpec(shape, index_map)


def _ab_layer(x, p_all, layer, hist_a, hist_b, g, w_in, a_w, b_w, ln_g, ln_b, w_out, gate, pproj,
              *, bb, tt, n_sub):
    nb, t, _ = x.shape
    assert nb % bb == 0 and t % tt == 0 and tt % n_sub == 0 and (bb == 1 or n_sub == 1)
    ts = tt // n_sub
    assert ts % CONV_ROWS == 0 and ts >= HIST_B
    msub = bb * ts
    grid = (nb // bb, t // tt)
    tok = lambda d: pl.BlockSpec((bb, tt, d), lambda b, s: (b, s, 0))
    state = lambda r: pl.BlockSpec((bb, r, D_BRANCH), lambda b, s: (b, 0, 0))
    return pl.pallas_call(
        functools.partial(_ab_layer_kernel, bb=bb, tt=tt, n_sub=n_sub),
        grid=grid,
        in_specs=[
            tok(D_MODEL),
            pl.BlockSpec((None, bb, tt, PLE_DIM), lambda b, s: (layer, b, s, 0)),
            state(HIST_A), state(HIST_B),
            _const_spec((1, D_MODEL)),
            _const_spec((D_MODEL, 7 * D_BRANCH)),
            _const_spec((CONV_A, D_BRANCH)),
            _const_spec((CONV_B, D_BRANCH)),
            _const_spec((1, D_BRANCH)), _const_spec((1, D_BRANCH)),
            _const_spec((2 * D_BRANCH, D_MODEL)),
            _const_spec((D_MODEL, D_MODEL)),
            _const_spec((PLE_DIM, D_MODEL)),
        ],
        out_specs=[tok(D_MODEL), state(HIST_A), state(HIST_B)],
        out_shape=[
            jax.ShapeDtypeStruct((nb, t, D_MODEL), F32),
            jax.ShapeDtypeStruct((nb, HIST_A, D_BRANCH), F32),
            jax.ShapeDtypeStruct((nb, HIST_B, D_BRANCH), F32),
        ],
        scratch_shapes=[
            pltpu.VMEM((SUBLANES - 1, PAD_B + ts - SUBLANES, D_BRANCH), F32),
            pltpu.VMEM((CONV_A, SUBLANES, D_BRANCH), F32),
            pltpu.VMEM((CONV_B, SUBLANES, D_BRANCH), F32),
        ] + [
            pltpu.VMEM(shape, dtype)
            for shape, dtype in [
                ((bb, PAD_A + ts, D_BRANCH), F32),
                ((bb, PAD_B + ts, D_BRANCH), F32),
                ((msub, 7 * D_BRANCH), F32),
                ((msub, 2 * D_BRANCH), BF16),
                ((msub, D_MODEL), BF16),
                ((msub, D_MODEL), F32),
            ]
            for _ in range(n_sub)
        ],
        compiler_params=pltpu.CompilerParams(
            dimension_semantics=("arbitrary", "arbitrary"),
            vmem_limit_bytes=VMEM_LIMIT_BYTES),
        name="ab_layer",
    )(x, p_all, hist_a, hist_b, g, w_in, a_w, b_w, ln_g, ln_b, w_out, gate, pproj)


def _cd_proj_kernel(h_ref, g_ref, win_ref, lng_ref, lnb_ref, ws_ref, cbt_ref,
                    cout_ref, q_ref, k_ref, v_ref, kb_ref, vb_ref, sdz_ref, *rest, bb, tt, chunk):
    cvn_s, gc_s = rest[-2:]
    m = bb * tt
    x = h_ref[...].reshape(m, D_MODEL)
    xb = _rms(x, g_ref[...]).astype(BF16)

    def proj(i):
        return _dot(xb, win_ref[:, i * D_BRANCH:(i + 1) * D_BRANCH])

    gc_s[...] = proj(0) * _silu(proj(2))
    cvn = _layernorm(proj(1), lng_ref[...], lnb_ref[...])
    if len(rest) == 3:
        rest[0][...] = cvn.reshape(bb, tt, D_BRANCH)
    cvn_s[...] = cvn.astype(BF16)
    q_ref[...] = (proj(3) * (D_HEAD_DIM ** -0.5)).astype(BF16).reshape(bb, tt, D_BRANCH)
    k = proj(4).reshape(bb, tt, D_BRANCH)
    k_ref[...] = k
    kb_ref[...] = k.astype(BF16)
    v = proj(5).reshape(bb, tt, D_BRANCH)
    v_ref[...] = v
    vb_ref[...] = v.astype(BF16)
    sdz_ref[...] = _silu(proj(6)).reshape(bb, tt, D_BRANCH)

    row = lax.broadcasted_iota(jnp.int32, (chunk, chunk), 0)
    col = lax.broadcasted_iota(jnp.int32, (chunk, chunk), 1)
    for hd in range(C_HEADS):
        lanes = slice(hd * C_HEAD_DIM, (hd + 1) * C_HEAD_DIM)
        w_tril = jnp.where(col <= row, ws_ref[hd], 0.0).astype(BF16)
        bias = cbt_ref[:, hd:hd + 1]
        for c0 in range(0, m, chunk):
            rows = pl.ds(c0, chunk)
            mixed = _dot(w_tril, cvn_s[rows, lanes]) + bias
            cout_ref[c0 // tt, pl.ds(c0 % tt, chunk), lanes] = (gc_s[rows, lanes] * mixed).astype(BF16)


def _cd_proj(h, g, w_in, ln_g, ln_b, ws, cbt, *, bb, tt, chunk, want_cvn):
    nb, t, _ = h.shape
    assert nb % bb == 0 and t % tt == 0 and tt % chunk == 0
    m = bb * tt
    grid = (nb // bb, t // tt)
    tok = lambda d: pl.BlockSpec((bb, tt, d), lambda b, s: (b, s, 0))
    shp = lambda dt: jax.ShapeDtypeStruct((nb, t, D_BRANCH), dt)
    return pl.pallas_call(
        functools.partial(_cd_proj_kernel, bb=bb, tt=tt, chunk=chunk),
        grid=grid,
        in_specs=[
            _stream_spec((bb, tt, D_MODEL), lambda b, s: (b, s, 0)),
            _const_spec((1, D_MODEL)),
            _const_spec((D_MODEL, 7 * D_BRANCH)),
            _const_spec((1, D_BRANCH)), _const_spec((1, D_BRANCH)),
            _const_spec((C_HEADS, chunk, chunk)),
            _const_spec((chunk, C_HEADS)),
        ],
        out_specs=[tok(D_BRANCH)] * (8 if want_cvn else 7),
        out_shape=[shp(BF16), shp(BF16), shp(F32), shp(F32), shp(BF16), shp(BF16), shp(F32)]
        + ([shp(F32)] if want_cvn else []),
        scratch_shapes=[
            pltpu.VMEM((m, D_BRANCH), BF16),
            pltpu.VMEM((m, D_BRANCH), F32),
        ],
        compiler_params=pltpu.CompilerParams(
            dimension_semantics=("arbitrary", "arbitrary"),
            vmem_limit_bytes=VMEM_LIMIT_BYTES),
        name="cd_proj",
    )(h, g, w_in, ln_g, ln_b, ws, cbt)


def _dot_nt(a, b):
    return lax.dot_general(a, b, (((1,), (1,)), ((), ())), preferred_element_type=F32)


def _pair_lanes(pr):
    return slice(pr * LANES, (pr + 1) * LANES)


def _sb_setup(q, qm_s, tri_s):
    tq = q.shape[0]
    lane = lax.broadcasted_iota(jnp.int32, (tq, LANES), 1)
    for hd in range(D_HEADS):
        pair = q[:, _pair_lanes(hd // 2)]
        in_head = (lane >= D_HEAD_DIM) if hd % 2 else (lane < D_HEAD_DIM)
        qm_s[hd * tq:(hd + 1) * tq, :] = jnp.where(in_head, pair, jnp.zeros_like(pair))
    n = tri_s.shape[1]
    row = lax.broadcasted_iota(jnp.int32, (n, n), 0)
    col = lax.broadcasted_iota(jnp.int32, (n, n), 1)
    tri = jnp.where(row > col, 1.0, 0.0).astype(BF16)
    tri_s[0:n, :] = tri
    tri_s[n:2 * n, :] = tri


def _sb_diag_mask(tq):
    shape = (2 * tq, KEY_BLOCK)
    r = lax.broadcasted_iota(jnp.int32, shape, 0)
    qi = jnp.where(r >= tq, r - tq, r)
    return lax.broadcasted_iota(jnp.int32, shape, 1) < qi


def _sb_pass(qk, av, nk, diag_valid, first, qm_s, acc_s, car_s, tri_s, lb_s, hl_s, a_s):
    m2 = qm_s.shape[0] // (D_HEADS // 2)
    pairs = [pl.ds(pr * m2, m2) for pr in range(D_HEADS // 2)]
    tri2 = jnp.concatenate([tri_s[0:nk, 0:nk], tri_s[2 * KEY_BLOCK:2 * KEY_BLOCK + nk, 0:nk]], axis=0)

    def masked(z):
        if diag_valid is None:
            return z
        return jnp.concatenate(
            [z[:, :nk - KEY_BLOCK], jnp.where(diag_valid, z[:, nk - KEY_BLOCK:], MASKED_SCORE)], axis=1)

    for pr, rows in enumerate(pairs):
        z = masked(qk(pr, qm_s[rows, :]))
        soft = jnp.log(1.0 + jnp.exp(-jnp.abs(z)))
        log_beta = jnp.minimum(z, 0.0) - soft
        log_keep = log_beta - z
        hi = log_keep.astype(BF16)
        hl_s[rows, 0:nk] = hi
        hl_s[rows, nk:2 * nk] = (log_keep - hi.astype(F32)).astype(BF16)
        total = jnp.sum(log_keep, axis=-1, keepdims=True)
        if first:
            lb_s[rows, 0:nk] = log_beta
            car_s[rows, :] = jnp.broadcast_to(total, (m2, LANES))
        else:
            carry = car_s[rows, :]
            lb_s[rows, 0:nk] = log_beta + jnp.concatenate([carry] * (nk // LANES), axis=1)
            car_s[rows, :] = carry + total
    for pr, rows in enumerate(pairs):
        later = _dot(hl_s[rows, 0:2 * nk], tri2)
        a_s[rows, 0:nk] = jnp.exp(lb_s[rows, 0:nk] + later).astype(BF16)
    for pr, rows in enumerate(pairs):
        pv = av(pr, a_s[rows, 0:nk])
        acc_s[rows, :] = pv if first else acc_s[rows, :] + pv


def _sb_all_spent(car_s):
    return (jnp.max(car_s[...]) < STICK_LOG_FLOOR).astype(jnp.int32)


def _sb_sweep(first_block, tail_pass, car_s):
    def cond(c):
        j, spent = c
        return jnp.logical_and(j >= 0, spent == 0)

    def body(c):
        j, _ = c
        tail_pass(j)
        return j - 1, _sb_all_spent(car_s)

    lax.while_loop(cond, body, (jnp.asarray(first_block, jnp.int32), _sb_all_spent(car_s)))


def _sb_finish(o_ref, gate_ref, acc_s):
    tq = acc_s.shape[0] // D_HEADS
    lane = lax.broadcasted_iota(jnp.int32, (tq, LANES), 1)
    for pr in range(D_HEADS // 2):
        even = acc_s[(2 * pr) * tq:(2 * pr + 1) * tq, :]
        odd = acc_s[(2 * pr + 1) * tq:(2 * pr + 2) * tq, :]
        o = jnp.where(lane < D_HEAD_DIM, even, odd)
        o_ref[0, :, _pair_lanes(pr)] = (o * gate_ref[0, :, _pair_lanes(pr)]).astype(o_ref.dtype)


def _key_rows(j):
    return pl.ds(pl.multiple_of(j * KEY_BLOCK, KEY_BLOCK), KEY_BLOCK)


def _sb_self_kernel(q_ref, k_ref, v_ref, gate_ref, o_ref, qm_s, acc_s, car_s, tri_s, lb_s, hl_s, a_s):
    i = pl.program_id(1)
    tq = q_ref.shape[1]
    scratch = (qm_s, acc_s, car_s, tri_s, lb_s, hl_s, a_s)
    _sb_setup(q_ref[0], qm_s, tri_s)
    prev, diag = _key_rows(jnp.maximum(i - 1, 0)), _key_rows(i)

    def keys(pr):
        return jnp.concatenate([k_ref[0, prev, _pair_lanes(pr)], k_ref[0, diag, _pair_lanes(pr)]], axis=0)

    def values(pr):
        v_prev = v_ref[0, prev, _pair_lanes(pr)]
        v_prev = jnp.where(jnp.broadcast_to(i, v_prev.shape) > 0, v_prev, jnp.zeros_like(v_prev))
        return jnp.concatenate([v_prev, v_ref[0, diag, _pair_lanes(pr)]], axis=0)

    _sb_pass(lambda pr, q: _dot_nt(q, keys(pr)), lambda pr, a: _dot(a, values(pr)),
             2 * KEY_BLOCK, _sb_diag_mask(tq), True, *scratch)

    def tail_pass(j):
        rows = _key_rows(j)
        _sb_pass(lambda pr, q: _dot_nt(q, k_ref[0, rows, _pair_lanes(pr)]),
                 lambda pr, a: _dot(a, v_ref[0, rows, _pair_lanes(pr)]),
                 KEY_BLOCK, None, False, *scratch)

    _sb_sweep(i - 2, tail_pass, car_s)
    _sb_finish(o_ref, gate_ref, acc_s)


def _sb_cached_kernel(q_ref, kn_ref, vn_ref, gate_ref, kl_ref, vl_ref, kc_hbm, vc_hbm, o_ref,
                      qm_s, acc_s, car_s, tri_s, lb_s, hl_s, a_s, kd_s, vd_s, kt_s, vt_s,
                      *, n_past_blocks):
    b = pl.program_id(0)
    tq = q_ref.shape[1]
    tn = kn_ref.shape[1]
    scratch = (qm_s, acc_s, car_s, tri_s, lb_s, hl_s, a_s)
    _sb_setup(q_ref[0], qm_s, tri_s)
    kd_s[...] = jnp.zeros_like(kd_s)
    vd_s[...] = jnp.zeros_like(vd_s)
    kd_s[0:tn, :] = kn_ref[0]
    vd_s[0:tn, :] = vn_ref[0]

    def pair_t(ref, pr):
        return ref[2 * pr:2 * pr + 2].reshape(LANES, KEY_BLOCK).astype(BF16)

    def qk_first(pr, q):
        return jnp.concatenate([_dot(q, pair_t(kl_ref.at[0], pr)), _dot_nt(q, kd_s[:, _pair_lanes(pr)])], axis=1)

    def av_first(pr, a):
        return _dot_nt(a[:, :KEY_BLOCK], pair_t(vl_ref.at[0], pr)) + _dot(a[:, KEY_BLOCK:], vd_s[:, _pair_lanes(pr)])

    _sb_pass(qk_first, av_first, 2 * KEY_BLOCK, _sb_diag_mask(tq), True, *scratch)

    def tail_pass(j):
        cols = _key_rows(j)
        pltpu.sync_copy(kc_hbm.at[b, :, :, cols], kt_s)
        pltpu.sync_copy(vc_hbm.at[b, :, :, cols], vt_s)
        _sb_pass(lambda pr, q: _dot(q, pair_t(kt_s, pr)), lambda pr, a: _dot_nt(a, pair_t(vt_s, pr)),
                 KEY_BLOCK, None, False, *scratch)

    _sb_sweep(n_past_blocks - 2, tail_pass, car_s)
    _sb_finish(o_ref, gate_ref, acc_s)


def _sb_scratch(tq):
    rows = D_HEADS * tq
    return [
        pltpu.VMEM((rows, LANES), BF16),
        pltpu.VMEM((rows, LANES), F32),
        pltpu.VMEM((rows, LANES), F32),
        pltpu.VMEM((4 * KEY_BLOCK, 2 * KEY_BLOCK), BF16),
        pltpu.VMEM((rows, 2 * KEY_BLOCK), F32),
        pltpu.VMEM((rows, 4 * KEY_BLOCK), BF16),
        pltpu.VMEM((rows, 2 * KEY_BLOCK), BF16),
    ]


def _sb_self(q, kb, vb, gate):
    nb, t, _ = q.shape
    assert t % KEY_BLOCK == 0
    full = pl.BlockSpec((1, t, D_BRANCH), lambda b, i: (b, 0, 0))
    blk = pl.BlockSpec((1, KEY_BLOCK, D_BRANCH), lambda b, i: (b, i, 0))
    return pl.pallas_call(
        _sb_self_kernel,
        grid=(nb, t // KEY_BLOCK),
        in_specs=[blk, full, full, blk],
        out_specs=blk,
        out_shape=jax.ShapeDtypeStruct((nb, t, D_BRANCH), BF16),
        scratch_shapes=_sb_scratch(KEY_BLOCK),
        compiler_params=pltpu.CompilerParams(
            dimension_semantics=("arbitrary", "arbitrary"),
            vmem_limit_bytes=VMEM_LIMIT_BYTES),
        name="sb_self",
    )(q, kb, vb, gate)


def _sb_cached(q, kb_new, vb_new, gate, k_cache, v_cache):
    nb, tq, _ = q.shape
    t_past = k_cache.shape[-1]
    assert tq <= KEY_BLOCK and tq % (2 * SUBLANES) == 0 and t_past % KEY_BLOCK == 0
    n_past_blocks = t_past // KEY_BLOCK
    new = pl.BlockSpec((1, tq, D_BRANCH), lambda b: (b, 0, 0))
    last = pl.BlockSpec((1, D_HEADS, D_HEAD_DIM, KEY_BLOCK), lambda b: (b, 0, 0, n_past_blocks - 1))
    anywhere = pl.BlockSpec(memory_space=pl.ANY)
    cache_blk = pltpu.VMEM((D_HEADS, D_HEAD_DIM, KEY_BLOCK), F32)
    return pl.pallas_call(
        functools.partial(_sb_cached_kernel, n_past_blocks=n_past_blocks),
        grid=(nb,),
        in_specs=[new, new, new, new, last, last, anywhere, anywhere],
        out_specs=new,
        out_shape=jax.ShapeDtypeStruct((nb, tq, D_BRANCH), BF16),
        scratch_shapes=_sb_scratch(tq) + [
            pltpu.VMEM((KEY_BLOCK, D_BRANCH), BF16),
            pltpu.VMEM((KEY_BLOCK, D_BRANCH), BF16),
            cache_blk, cache_blk,
        ],
        compiler_params=pltpu.CompilerParams(
            dimension_semantics=("arbitrary",),
            vmem_limit_bytes=VMEM_LIMIT_BYTES),
        name="sb_cached",
    )(q, kb_new, vb_new, gate, k_cache, v_cache, k_cache, v_cache)


def _cd_out_kernel(h_ref, cout_ref, dout_ref, p_ref, wout_ref, gate_ref, pproj_ref, fg_ref,
                   y_ref, *, bb, tt):
    m = bb * tt
    x = h_ref[...].reshape(m, D_MODEL)
    d_out = dout_ref[...].reshape(m, D_BRANCH)
    c_out = cout_ref[...].reshape(m, D_BRANCH)
    h = x + _dot(c_out, wout_ref[0:D_BRANCH, :]) + _dot(d_out, wout_ref[D_BRANCH:, :])
    gate = jax.nn.sigmoid(_dot(h.astype(BF16), gate_ref[...]))
    emb = _dot(p_ref[...].reshape(m, PLE_DIM).astype(BF16), pproj_ref[...])
    y_ref[...] = _rms(h + gate * emb, fg_ref[...]).reshape(bb, tt, D_MODEL)


def _cd_out(h, cout, dout, p_all, layer, w_out, gate, pproj, fg, *, bb, tt):
    nb, t, _ = h.shape
    grid = (nb // bb, t // tt)
    tok = lambda d: pl.BlockSpec((bb, tt, d), lambda b, s: (b, s, 0))
    tok_in = lambda d: _stream_spec((bb, tt, d), lambda b, s: (b, s, 0))
    return pl.pallas_call(
        functools.partial(_cd_out_kernel, bb=bb, tt=tt),
        grid=grid,
        in_specs=[
            tok_in(D_MODEL), tok_in(D_BRANCH), tok_in(D_BRANCH),
            _stream_spec((None, bb, tt, PLE_DIM), lambda b, s: (layer, b, s, 0)),
            _const_spec((2 * D_BRANCH, D_MODEL)),
            _const_spec((D_MODEL, D_MODEL)),
            _const_spec((PLE_DIM, D_MODEL)),
            _const_spec((1, D_MODEL)),
        ],
        out_specs=tok(D_MODEL),
        out_shape=jax.ShapeDtypeStruct((nb, t, D_MODEL), F32),
        compiler_params=pltpu.CompilerParams(
            dimension_semantics=("arbitrary", "arbitrary"),
            vmem_limit_bytes=VMEM_LIMIT_BYTES),
        name="cd_out",
    )(h, cout, dout, p_all, w_out, gate, pproj, fg)


def _trunk(x, p_all, hist_a, hist_b, k_cache, v_cache, w, *, bb, tt, n_sub):
    t = x.shape[1]
    chunk = min(t, C_CHUNK)
    h1, new_a, new_b = _ab_layer(
        x, p_all, 0, hist_a, hist_b, w["norm_g"][0], w["ab_w_in"], w["a_conv_w"], w["b_conv_w"],
        w["b_ln_g"], w["b_ln_b"], w["ab_w_out"], w["ple_gate"][0], w["ple_proj"][0],
        bb=bb, tt=tt, n_sub=n_sub)
    tt1 = 2 * tt if bb == 1 else tt
    cout, q, k, v, kb, vb, sdz, *cvn = _cd_proj(
        h1, w["norm_g"][1], w["cd_w_in"], w["c_ln_g"], w["c_ln_b"],
        w["c_ws"][:, :chunk, :chunk], w["c_bt"][:chunk], bb=bb, tt=tt1, chunk=chunk,
        want_cvn=k_cache is not None)
    if k_cache is None:
        dout = _sb_self(q, kb, vb, sdz)
    else:
        dout = _sb_cached(q, kb, vb, sdz, k_cache, v_cache)
    y = _cd_out(h1, cout, dout, p_all, 1, w["cd_w_out"], w["ple_gate"][1], w["ple_proj"][1],
                w["final_g"], bb=bb, tt=tt1)
    return y, new_a, new_b, (cvn[0] if cvn else None), k, v


def kernel(x_prompt, x_sample, state_a_conv, state_b_conv, cache_d_k, cache_d_v, p_prompt, p_sample,
           norm_g, ple_gate, ple_proj, ab_w_in, a_conv_w, b_conv_w, b_ln_g, b_ln_b, ab_w_out,
           cd_w_in, c_ln_g, c_ln_b, c_ws, c_b, cd_w_out, final_g):
    bp, tp, _ = x_prompt.shape
    bs, ts, _ = x_sample.shape
    w = {
        "norm_g": norm_g.reshape(2, 1, D_MODEL),
        "ple_gate": ple_gate.astype(BF16),
        "ple_proj": ple_proj.astype(BF16),
        "ab_w_in": ab_w_in[0].astype(BF16),
        "a_conv_w": a_conv_w[0],
        "b_conv_w": b_conv_w[0],
        "b_ln_g": b_ln_g, "b_ln_b": b_ln_b,
        "ab_w_out": ab_w_out[0].astype(BF16),
        "cd_w_in": cd_w_in[0].astype(BF16),
        "c_ln_g": c_ln_g, "c_ln_b": c_ln_b,
        "c_ws": c_ws[0],
        "c_bt": c_b[0].T,
        "cd_w_out": cd_w_out[0].astype(BF16),
        "final_g": final_g.reshape(1, D_MODEL),
    }
    zeros_a = jnp.zeros((bp, HIST_A, D_BRANCH), F32)
    zeros_b = jnp.zeros((bp, HIST_B, D_BRANCH), F32)
    yp, pa, pb, _, pk, pv = _trunk(x_prompt, p_prompt, zeros_a, zeros_b, None, None, w,
                                   bb=1, tt=512, n_sub=2)
    k_cache = jnp.transpose(cache_d_k[0], (0, 2, 3, 1))
    v_cache = jnp.transpose(cache_d_v[0], (0, 2, 3, 1))
    ys, sa, sb, scv, sk, sv = _trunk(x_sample, p_sample, state_a_conv[0], state_b_conv[0],
                                     k_cache, v_cache, w, bb=8, tt=ts, n_sub=1)
    heads = lambda a: a.reshape(1, a.shape[0], a.shape[1], D_HEADS, D_HEAD_DIM)
    return (yp, ys, pa[None], sa[None], pb[None], sb[None], scv[None],
            heads(pk), heads(pv), heads(sk), heads(sv))
```
